```python
import math
import jax
import jax.numpy as jnp
from jax import lax
import numpy as np

D_MODEL = 1024
BATCH = 32
SEQ = 2048
DEPTH = 2
DEC_BATCH = 128
DEC_SEQ = 1
PAST_LEN = 16384
PAGE_SIZE = 128

HEAD_DIM = 64
GROUP_HEADS = 4
GROUP_WIDTH = GROUP_HEADS * HEAD_DIM
MIX_WIDTH = 4 * GROUP_WIDTH
ROPE_THETA = 500000.0
ROT_FRAC = 4
Q_BLOCK = 128
NEG = -1e30
BIG = 1e9
EPS = 1e-6
MLA_Q_RANK = 192
MLA_KV_RANK = 128
MLA_NOPE = 64
MLA_ROPE = 32
MLA_V = HEAD_DIM
MLA_SCALE = (MLA_NOPE + MLA_ROPE) ** -0.5
DIFF_DK = 32
DIFF_DV = 2 * DIFF_DK
NSA_DK = HEAD_DIM
NSA_CMP_BLOCK = 32
NSA_CMP_STRIDE = 16
NSA_SEL_BLOCK = 64
NSA_TOPN = 16
NSA_LOCAL = 2
NSA_WINDOW = 512
DSA_DK = HEAD_DIM
DSA_IDX_HEADS = 4
DSA_IDX_DIM = 32
DSA_TOPK = 256
DSA_IDX_SCALE = (DSA_IDX_DIM ** -0.5) * (DSA_IDX_HEADS ** -0.5)

IN_SPLITS = (
    ('mla_cq', MLA_Q_RANK), ('mla_ckv', MLA_KV_RANK), ('mla_kr', MLA_ROPE), ('mla_z', GROUP_WIDTH),
    ('diff_q', GROUP_HEADS * 2 * DIFF_DK), ('diff_k', 2 * DIFF_DK), ('diff_v', DIFF_DV), ('diff_z', GROUP_WIDTH),
    ('nsa_q', GROUP_HEADS * NSA_DK), ('nsa_cmp', 2 * NSA_DK), ('nsa_sel', 2 * NSA_DK), ('nsa_win', 2 * NSA_DK),
    ('nsa_gate', 3 * GROUP_HEADS), ('nsa_z', GROUP_WIDTH),
    ('dsa_q', GROUP_HEADS * DSA_DK), ('dsa_kv', 2 * DSA_DK), ('dsa_iq', DSA_IDX_HEADS * DSA_IDX_DIM),
    ('dsa_iw', DSA_IDX_HEADS), ('dsa_ik', DSA_IDX_DIM), ('dsa_z', GROUP_WIDTH),
)
IN_WIDTH = sum(w for _, w in IN_SPLITS)

kernel_name = 'hybrid_mla_diff_nsa_dsa_step'


def rms_norm(x, g):
    xf = x.astype(jnp.float32)
    y = xf * lax.rsqrt(jnp.mean(xf * xf, -1, keepdims=True) + EPS)
    return (y * g.astype(jnp.float32)).astype(x.dtype)


def apply_rope(x, pos, rot_dim):
    half = rot_dim // 2
    inv = ROPE_THETA ** (-jnp.arange(half, dtype=jnp.float32) * 2.0 / rot_dim)
    ang = pos.astype(jnp.float32)[:, None] * inv
    ang = ang.reshape(ang.shape[:1] + (1,) * (x.ndim - 3) + (half,))
    cos, sin = jnp.cos(ang), jnp.sin(ang)
    xf = x.astype(jnp.float32)
    x1, x2, rest = xf[..., :half], xf[..., half:rot_dim], xf[..., rot_dim:]
    return jnp.concatenate([x1 * cos - x2 * sin, x1 * sin + x2 * cos, rest], -1).astype(x.dtype)


def rope_kv(kv, pos, dk, rot_dim):
    return jnp.concatenate([apply_rope(kv[..., :dk], pos, rot_dim), kv[..., dk:]], -1)


def masked_softmax(s, mask):
    s = jnp.where(mask, s.astype(jnp.float32), NEG)
    m = jnp.max(s, -1, keepdims=True)
    p = jnp.exp(s - m) * mask
    return p / jnp.maximum(jnp.sum(p, -1, keepdims=True), 1e-30)


def split_cols(p):
    out, off = {}, 0
    for name, w in IN_SPLITS:
        out[name] = p[..., off:off + w]
        off += w
    return out


def paged_past(cache, l, page_table):
    rows = cache[l, page_table]
    return rows.reshape(rows.shape[0], -1, rows.shape[-1])


def gather_paged(cache, l, page_table, new_rows, pos):
    b = jnp.arange(pos.shape[0])[:, None, None]
    page = jnp.minimum(pos // PAGE_SIZE, page_table.shape[1] - 1)
    past = cache[l, page_table[b, page], pos % PAGE_SIZE]
    new = new_rows[b, jnp.clip(pos - PAST_LEN, 0, new_rows.shape[1] - 1)]
    return jnp.where((pos < PAST_LEN)[..., None], past, new)


def gather_local(rows, pos):
    b = jnp.arange(rows.shape[0])[:, None, None]
    return rows[b, jnp.minimum(pos, rows.shape[1] - 1)]


def nsa_compress(rows, pos_emb, w1, w2):
    B, L, _ = rows.shape
    r = NSA_CMP_BLOCK // NSA_CMP_STRIDE
    n_chunk = L // NSA_CMP_STRIDE
    n_cmp = n_chunk - r + 1
    chunks = rows[:, :n_chunk * NSA_CMP_STRIDE].reshape(B, n_chunk, NSA_CMP_STRIDE, 2, NSA_DK)
    w1r = w1.reshape(2, r, NSA_CMP_STRIDE, NSA_DK, NSA_DK)
    pre = jnp.einsum('csd,csdh->ch', pos_emb, w1)
    for j in range(r):
        pre = pre + jnp.einsum('bnsce,csed->bncd', chunks, w1r[:, j])[:, j:j + n_cmp]
    out = jnp.einsum('bnch,chk->bnck', jax.nn.silu(pre), w2)
    ends = jnp.arange(n_cmp) * NSA_CMP_STRIDE + NSA_CMP_BLOCK - 1
    return out[:, :, 0], out[:, :, 1], ends


def nsa_overlap(n_cmp, n_sel):
    cs = jnp.arange(n_cmp) * NSA_CMP_STRIDE
    ce = cs + NSA_CMP_BLOCK - 1
    ss = jnp.arange(n_sel) * NSA_SEL_BLOCK
    se = ss + NSA_SEL_BLOCK - 1
    return ((cs[:, None] <= se[None, :]) & (ce[:, None] >= ss[None, :])).astype(jnp.float32)


def trunk_layer(x, c, pos, l, wts, past):
    B, S, _ = x.shape
    H = GROUP_HEADS
    dt = x.dtype
    mod = c @ wts['w_ada'][l] + wts['b_ada'][l]
    shift, scale, gate = jnp.split(mod[:, None, :], 3, axis=-1)
    h = rms_norm(x, wts['norm_g'][l]) * (1 + scale) + shift
    p = split_cols(h @ wts['w_in'][l])

    mla_c_new = rms_norm(p['mla_ckv'], wts['mla_kv_norm'][l])
    mla_r_new = apply_rope(p['mla_kr'], pos, MLA_ROPE)
    mla_row = jnp.concatenate([mla_c_new, mla_r_new], -1)
    qf = jnp.einsum('bsr,rhd->bshd', rms_norm(p['mla_cq'], wts['mla_q_norm'][l]), wts['mla_w_uq'][l])
    w_uv = wts['mla_w_uv'][l]
    q_lat = jnp.einsum('bshn,khn->bshk', qf[..., :MLA_NOPE], wts['mla_w_uk'][l])
    q_pe = apply_rope(qf[..., MLA_NOPE:], pos, MLA_ROPE)
    d_rot = DIFF_DK // ROT_FRAC
    dq = apply_rope(p['diff_q'].reshape(B, S, H * 2, DIFF_DK), pos, d_rot).reshape(B, S, H, 2, DIFF_DK)
    dk_new = apply_rope(p['diff_k'].reshape(B, S, 2, DIFF_DK), pos, d_rot).reshape(B, S, 2 * DIFF_DK)
    diff_row = jnp.concatenate([dk_new, p['diff_v']], -1)
    lam_p = wts['diff_lambda'][l].astype(jnp.float32)
    lam_init = 0.8 - 0.6 * math.exp(-0.3 * l)
    lam = jnp.exp(jnp.sum(lam_p[0] * lam_p[1])) - jnp.exp(jnp.sum(lam_p[2] * lam_p[3])) + lam_init
    diff_subln = wts['diff_subln'][l]
    n_rot = NSA_DK // ROT_FRAC
    nq = p['nsa_q'].reshape(B, S, H, NSA_DK)
    nq_rot = apply_rope(nq, pos, n_rot)
    cmp_row = p['nsa_cmp']
    sel_row = rope_kv(p['nsa_sel'], pos, NSA_DK, n_rot)
    win_row = rope_kv(p['nsa_win'], pos, NSA_DK, n_rot)
    ngate = jax.nn.sigmoid(p['nsa_gate'].astype(jnp.float32)).reshape(B, S, 3, H)
    s_rot = DSA_DK // ROT_FRAC
    i_rot = DSA_IDX_DIM // ROT_FRAC
    sq = apply_rope(p['dsa_q'].reshape(B, S, H, DSA_DK), pos, s_rot)
    dsa_kv_row = rope_kv(p['dsa_kv'], pos, DSA_DK, s_rot)
    iq = apply_rope(p['dsa_iq'].reshape(B, S, DSA_IDX_HEADS, DSA_IDX_DIM), pos, i_rot)
    dsa_idx_row = apply_rope(p['dsa_ik'], pos, i_rot)

    if past is None:
        L = S
        kpos = pos
        mla_c, mla_r = mla_c_new, mla_r_new
        diff_all, cmp_all, idx_all = diff_row, cmp_row, dsa_idx_row
        sel_gather = lambda ps: gather_local(sel_row, ps)
        dsa_gather = lambda ps: gather_local(dsa_kv_row, ps)
        win_pad = jnp.pad(win_row, ((0, 0), (NSA_WINDOW, 0), (0, 0)))
        span = NSA_WINDOW + Q_BLOCK

        def win_keys(q0):
            return (lax.dynamic_slice_in_dim(win_pad, q0, span, axis=1),
                    q0 - NSA_WINDOW + jnp.arange(span))
        win_state = win_row[:, S - min(NSA_WINDOW, S):]
    else:
        pt = past['page_table']
        L = PAST_LEN + S
        kpos = jnp.arange(L)
        mla_all = jnp.concatenate([paged_past(past['cache_mla'], l, pt), mla_row], 1)
        mla_c, mla_r = mla_all[..., :MLA_KV_RANK], mla_all[..., MLA_KV_RANK:]
        diff_all = jnp.concatenate([paged_past(past['cache_diff'], l, pt), diff_row], 1)
        cmp_all = jnp.concatenate([paged_past(past['cache_nsa_cmp'], l, pt), cmp_row], 1)
        idx_all = jnp.concatenate([paged_past(past['cache_dsa_idx'], l, pt), dsa_idx_row], 1)
        sel_gather = lambda ps: gather_paged(past['cache_nsa_sel'], l, pt, sel_row, ps)
        dsa_gather = lambda ps: gather_paged(past['cache_dsa_kv'], l, pt, dsa_kv_row, ps)
        buf = past['state_nsa_win'][l]
        wb = buf.shape[1]
        win_all = jnp.concatenate([buf, win_row], 1)
        win_pos = PAST_LEN - wb + jnp.arange(wb + S)
        win_keys = lambda q0: (win_all, win_pos)
        win_state = win_all[:, S:]
    diff_k_all = diff_all[..., :2 * DIFF_DK].reshape(B, L, 2, DIFF_DK)
    diff_v_all = diff_all[..., 2 * DIFF_DK:]
    cmp_k, cmp_v, cmp_end = nsa_compress(cmp_all, wts['nsa_cmp_pos'][l], wts['nsa_cmp_w1'][l], wts['nsa_cmp_w2'][l])
    n_sel = -(-L // NSA_SEL_BLOCK)
    overlap = nsa_overlap(cmp_k.shape[1], n_sel)
    n_top = min(NSA_TOPN, n_sel)
    k_sel = min(DSA_TOPK, L // 4)

    def mix_block(qs, qpos, q0):
        Q = qpos.shape[0]
        causal = kpos[None, :] <= qpos[:, None]
        s = (jnp.einsum('bqhk,blk->bhql', qs['q_lat'], mla_c) +
             jnp.einsum('bqhr,blr->bhql', qs['q_pe'], mla_r)) * MLA_SCALE
        pr = masked_softmax(s, causal).astype(dt)
        o_lat = jnp.einsum('bhql,blk->bqhk', pr, mla_c)
        o_a = jnp.einsum('bqhk,khv->bqhv', o_lat, w_uv).reshape(B, Q, -1)
        s = jnp.einsum('bqhcd,blcd->bchql', qs['dq'], diff_k_all) * DIFF_DK ** -0.5
        pr = masked_softmax(s, causal)
        a = (pr[:, 0] - lam * pr[:, 1]).astype(dt)
        o_b = jnp.einsum('bhql,bld->bqhd', a, diff_v_all)
        o_b = (rms_norm(o_b, diff_subln) * (1.0 - lam_init)).reshape(B, Q, -1)
        s = jnp.einsum('bqhd,bnd->bhqn', qs['nq'], cmp_k) * NSA_DK ** -0.5
        pc = masked_softmax(s, cmp_end[None, :] <= qpos[:, None])
        o_cmp = jnp.einsum('bhqn,bnd->bqhd', pc.astype(dt), cmp_v)
        imp = jnp.einsum('bhqn,nj->bqj', pc, overlap)
        jb = jnp.arange(n_sel)[None, :]
        qblk = (qpos // NSA_SEL_BLOCK)[:, None]
        forced = (jb == 0) | (jb > qblk - NSA_LOCAL)
        score = jnp.where(jb > qblk, NEG, jnp.where(forced, BIG, imp))
        _, top = lax.top_k(score, n_top)
        spos = (top[..., None] * NSA_SEL_BLOCK + jnp.arange(NSA_SEL_BLOCK)).reshape(B, Q, -1)
        skv = sel_gather(spos)
        s = jnp.einsum('bqhd,bqkd->bhqk', qs['nq_rot'], skv[..., :NSA_DK]) * NSA_DK ** -0.5
        ps = masked_softmax(s, (spos <= qpos[None, :, None])[:, None]).astype(dt)
        o_sel = jnp.einsum('bhqk,bqkd->bqhd', ps, skv[..., NSA_DK:])
        wkv, wpos = win_keys(q0)
        wmask = ((wpos[None, :] <= qpos[:, None]) & (wpos[None, :] > qpos[:, None] - NSA_WINDOW)
                 & (wpos[None, :] >= 0))
        s = jnp.einsum('bqhd,bld->bhql', qs['nq_rot'], wkv[..., :NSA_DK]) * NSA_DK ** -0.5
        pw = masked_softmax(s, wmask).astype(dt)
        o_win = jnp.einsum('bhql,bld->bqhd', pw, wkv[..., NSA_DK:])
        g = qs['ngate']
        o_c = (g[:, :, 0, :, None] * o_cmp + g[:, :, 1, :, None] * o_sel
               + g[:, :, 2, :, None] * o_win).reshape(B, Q, -1)
        isc = jax.nn.relu(jnp.einsum('bqhd,bld->bqhl', qs['iq'], idx_all).astype(jnp.float32))
        isc = jnp.einsum('bqhl,bqh->bql', isc, qs['iw'].astype(jnp.float32)) * DSA_IDX_SCALE
        isc = jnp.where(causal[None], isc, NEG)
        _, tk = lax.top_k(isc, k_sel)
        kv = dsa_gather(tk)
        s = jnp.einsum('bqhd,bqkd->bhqk', qs['sq'], kv[..., :DSA_DK]) * DSA_DK ** -0.5
        pd = masked_softmax(s, (tk <= qpos[None, :, None])[:, None]).astype(dt)
        o_d = jnp.einsum('bhqk,bqkd->bqhd', pd, kv[..., DSA_DK:]).reshape(B, Q, -1)
        return jnp.concatenate([o_a, o_b, o_c, o_d], -1).astype(dt)

    qside = {'q_lat': q_lat, 'q_pe': q_pe, 'dq': dq, 'nq': nq, 'nq_rot': nq_rot, 'ngate': ngate,
             'sq': sq, 'iq': iq, 'iw': p['dsa_iw']}
    if past is None:
        def block_fn(q0):
            qs = {k: lax.dynamic_slice_in_dim(v, q0, Q_BLOCK, axis=1) for k, v in qside.items()}
            return mix_block(qs, lax.dynamic_slice_in_dim(pos, q0, Q_BLOCK), q0)
        blocks = lax.map(block_fn, jnp.arange(S // Q_BLOCK) * Q_BLOCK)
        o = jnp.moveaxis(blocks, 0, 1).reshape(B, S, MIX_WIDTH)
    else:
        o = mix_block(qside, pos, 0)

    z = jnp.concatenate([p['mla_z'], p['diff_z'], p['nsa_z'], p['dsa_z']], -1)
    y = (o * jax.nn.silu(z)) @ wts['w_out'][l]
    x = x + gate * y
    rows = {'mla': mla_row, 'diff': diff_row, 'nsa_cmp': cmp_row, 'nsa_sel': sel_row,
            'dsa_kv': dsa_kv_row, 'dsa_idx': dsa_idx_row, 'nsa_win': win_state}
    return x, rows


def setup_inputs(seed: int = 0) -> dict:
    key = jax.random.key(seed)
    ks = iter(jax.random.split(key, 32))
    f32 = jnp.float32

    def nrm(shape, scale=1.0):
        return jax.random.normal(next(ks), shape, f32) * scale
    n_pages = PAST_LEN // PAGE_SIZE
    n_pool = (DEC_BATCH * n_pages * 5) // 4
    win_buf = min(NSA_WINDOW, PAST_LEN)
    H = GROUP_HEADS
    page_table = jax.random.permutation(next(ks), n_pool)[:DEC_BATCH * n_pages]
    page_table = page_table.reshape(DEC_BATCH, n_pages).astype(jnp.int32)
    return {
        'x_prompt': nrm((BATCH, SEQ, D_MODEL)),
        'x_sample': nrm((DEC_BATCH, DEC_SEQ, D_MODEL)),
        'cache_mla': nrm((DEPTH, n_pool, PAGE_SIZE, MLA_KV_RANK + MLA_ROPE)),
        'cache_diff': nrm((DEPTH, n_pool, PAGE_SIZE, 2 * DIFF_DK + DIFF_DV)),
        'cache_nsa_cmp': nrm((DEPTH, n_pool, PAGE_SIZE, 2 * NSA_DK)),
        'cache_nsa_sel': nrm((DEPTH, n_pool, PAGE_SIZE, 2 * NSA_DK)),
        'cache_dsa_kv': nrm((DEPTH, n_pool, PAGE_SIZE, 2 * DSA_DK)),
        'cache_dsa_idx': nrm((DEPTH, n_pool, PAGE_SIZE, DSA_IDX_DIM)),
        'state_nsa_win': nrm((DEPTH, DEC_BATCH, win_buf, 2 * NSA_DK)),
        'page_table': page_table,
        'c_prompt': nrm((BATCH, D_MODEL)),
        'c_sample': nrm((DEC_BATCH, D_MODEL)),
        'w_ada': nrm((DEPTH, D_MODEL, 3 * D_MODEL), 0.2 * D_MODEL ** -0.5),
        'b_ada': nrm((DEPTH, 3 * D_MODEL), 0.02),
        'norm_g': 1.0 + nrm((DEPTH, D_MODEL), 0.02),
        'w_in': nrm((DEPTH, D_MODEL, IN_WIDTH), D_MODEL ** -0.5),
        'mla_q_norm': 1.0 + nrm((DEPTH, MLA_Q_RANK), 0.02),
        'mla_kv_norm': 1.0 + nrm((DEPTH, MLA_KV_RANK), 0.02),
        'mla_w_uq': nrm((DEPTH, MLA_Q_RANK, H, MLA_NOPE + MLA_ROPE), MLA_Q_RANK ** -0.5),
        'mla_w_uk': nrm((DEPTH, MLA_KV_RANK, H, MLA_NOPE), MLA_KV_RANK ** -0.5),
        'mla_w_uv': nrm((DEPTH, MLA_KV_RANK, H, MLA_V), MLA_KV_RANK ** -0.5),
        'diff_lambda': nrm((DEPTH, 4, DIFF_DK), 0.1),
        'diff_subln': 1.0 + nrm((DEPTH, DIFF_DV), 0.02),
        'nsa_cmp_pos': nrm((DEPTH, 2, NSA_CMP_BLOCK, NSA_DK), 0.1),
        'nsa_cmp_w1': nrm((DEPTH, 2, NSA_CMP_BLOCK, NSA_DK, NSA_DK), (NSA_CMP_BLOCK * NSA_DK) ** -0.5),
        'nsa_cmp_w2': nrm((DEPTH, 2, NSA_DK, NSA_DK), NSA_DK ** -0.5),
        'w_out': nrm((DEPTH, MIX_WIDTH, D_MODEL), MIX_WIDTH ** -0.5),
        'final_norm': 1.0 + nrm((D_MODEL,), 0.02),
    }


def reference(x_prompt, x_sample, cache_mla, cache_diff, cache_nsa_cmp, cache_nsa_sel, cache_dsa_kv,
              cache_dsa_idx, state_nsa_win, page_table, c_prompt, c_sample, w_ada, b_ada, norm_g, w_in,
              mla_q_norm, mla_kv_norm, mla_w_uq, mla_w_uk, mla_w_uv, diff_lambda, diff_subln,
              nsa_cmp_pos, nsa_cmp_w1, nsa_cmp_w2, w_out, final_norm):
    wts = {'w_ada': w_ada, 'b_ada': b_ada, 'norm_g': norm_g, 'w_in': w_in, 'mla_q_norm': mla_q_norm,
           'mla_kv_norm': mla_kv_norm, 'mla_w_uq': mla_w_uq, 'mla_w_uk': mla_w_uk, 'mla_w_uv': mla_w_uv,
           'diff_lambda': diff_lambda, 'diff_subln': diff_subln, 'nsa_cmp_pos': nsa_cmp_pos,
           'nsa_cmp_w1': nsa_cmp_w1, 'nsa_cmp_w2': nsa_cmp_w2, 'w_out': w_out}
    past = {'cache_mla': cache_mla, 'cache_diff': cache_diff, 'cache_nsa_cmp': cache_nsa_cmp,
            'cache_nsa_sel': cache_nsa_sel, 'cache_dsa_kv': cache_dsa_kv, 'cache_dsa_idx': cache_dsa_idx,
            'state_nsa_win': state_nsa_win, 'page_table': page_table}
    pos_p = jnp.arange(x_prompt.shape[1])
    pos_s = PAST_LEN + jnp.arange(x_sample.shape[1])
    xp, xs = x_prompt, x_sample
    rows_p, rows_s = [], []
    for l in range(DEPTH):
        xp, rp = trunk_layer(xp, c_prompt, pos_p, l, wts, None)
        xs, rs = trunk_layer(xs, c_sample, pos_s, l, wts, past)
        rows_p.append(rp)
        rows_s.append(rs)

    def stack(rows, name):
        return jnp.stack([r[name] for r in rows])
    y_prompt = rms_norm(xp, final_norm)
    y_sample = rms_norm(xs, final_norm)
    return (y_prompt, y_sample,
            stack(rows_p, 'mla'), stack(rows_s, 'mla'),
            stack(rows_p, 'diff'), stack(rows_s, 'diff'),
            stack(rows_p, 'nsa_cmp'), stack(rows_s, 'nsa_cmp'),
            stack(rows_p, 'nsa_sel'), stack(rows_s, 'nsa_sel'),
            stack(rows_p, 'dsa_kv'), stack(rows_s, 'dsa_kv'),
            stack(rows_p, 'dsa_idx'), stack(rows_s, 'dsa_idx'),
            stack(rows_p, 'nsa_win'), stack(rows_s, 'nsa_win'))
```

```python
import functools
import math

import numpy as np
import jax
import jax.numpy as jnp
from jax import lax
from jax.experimental import pallas as pl
from jax.experimental.pallas import tpu as pltpu

F32 = jnp.float32
BF = jnp.bfloat16

HEAD_DIM = 64
HEADS = 4
GROUP_WIDTH = HEADS * HEAD_DIM
ROPE_THETA = 500000.0
NEG = -1e30
BIG = 1e9
EPS = 1e-6
MLA_Q_RANK = 192
MLA_KV_RANK = 128
MLA_NOPE = 64
MLA_ROPE = 32
MLA_SCALE = (MLA_NOPE + MLA_ROPE) ** -0.5
DIFF_DK = 32
NSA_DK = 64
NSA_CMP_BLOCK = 32
NSA_CMP_STRIDE = 16
NSA_SEL_BLOCK = 64
NSA_TOPN = 16
NSA_LOCAL = 2
NSA_WINDOW = 512
DSA_DK = 64
DSA_IDX_DIM = 32
DSA_TOPK = 256
DSA_IDX_SCALE = (DSA_IDX_DIM ** -0.5) * (HEADS ** -0.5)

LANES = 128
VMEM_LIMIT = 56 * 1024 * 1024
INT_MIN = -2 ** 31

P_CQ, P_CKV, P_MISC, P_Z, P_DQ, P_DIFFKV, P_NQ, P_CMP, P_SEL, P_WIN, P_SQ, P_DSAKV, P_IQ, P_IK4, P_END = (
    0, 256, 384, 512, 1536, 1792, 1920, 2176, 2304, 2432, 2560, 2816, 2944, 3072, 3200)
M_GATE, M_IW = 32, 44
T_QPE, T_H32, T_DIFFKV, T_H64, T_KV64, T_MISC, T_END = 0, 128, 384, 512, 768, 896, 1024


def _dot(a, b):
    return jnp.dot(a, b, preferred_element_type=F32)


def _dot_t(a, b):
    return lax.dot_general(a, b, (((1,), (1,)), ((), ())), preferred_element_type=F32)


def _silu(x):
    return x / (1.0 + jnp.exp(-x))


def _const_spec(a):
    nd = a.ndim
    return pl.BlockSpec(a.shape, lambda *_: (0,) * nd)


def _params(sem):
    return pltpu.CompilerParams(dimension_semantics=sem, vmem_limit_bytes=VMEM_LIMIT)


def _ada_body(c_ref, w_ref, b_ref, o_ref):
    o_ref[...] = _dot(c_ref[...].astype(BF), w_ref[...]) + b_ref[...]


def _ada(c, w_bf, b):
    return pl.pallas_call(
        _ada_body,
        out_shape=jax.ShapeDtypeStruct((c.shape[0], w_bf.shape[1]), F32),
        compiler_params=pltpu.CompilerParams(vmem_limit_bytes=VMEM_LIMIT),
        name="ada_mod",
    )(c, w_bf, b)


def _rope(x, tab_ref, off, half):
    w = x.shape[1]
    c = tab_ref[0, :, off:off + w]
    sa = tab_ref[1, :, off:off + w]
    sb = tab_ref[2, :, off:off + w]
    return x * c + pltpu.roll(x, w - half, 1) * sa + pltpu.roll(x, half, 1) * sb


def _inproj_body(x_ref, mod_ref, g_ref, w_ref, qn_ref, kvn_ref, wuqn_ref, wuqp_ref, wuk_ref, tab_ref,
                 qa_ref, dq_ref, nq_ref, nqr_ref, sq_ref, iq_ref, misc_ref, z_ref,
                 mla_ref, diff_ref, cmp_ref, sel_ref, win_ref, kv_ref, idx_ref,
                 mlab_ref, diffb_ref, selb_ref, winb_ref, kvb_ref, ik4b_ref):
    d = x_ref.shape[-1]
    x = x_ref[...]
    mod = mod_ref[...]
    shift, scale = mod[:, 0:d], mod[:, d:2 * d]
    xn = x * lax.rsqrt(jnp.mean(x * x, -1, keepdims=True) + EPS) * g_ref[...]
    hb = (xn * (1.0 + scale) + shift).astype(BF)
    lane = lax.broadcasted_iota(jnp.int32, (x.shape[0], LANES), 1)

    def proj(a, b):
        return _dot(hb, w_ref[:, a:b])

    cq = proj(P_CQ, P_CKV)
    cqn = (cq * lax.rsqrt(jnp.sum(cq * cq, -1, keepdims=True) * (1.0 / MLA_Q_RANK) + EPS) * qn_ref[...]).astype(BF)
    q_lat = _dot(_dot(cqn, wuqn_ref[...]).astype(BF), wuk_ref[...])
    q_pe = _rope(_dot(cqn, wuqp_ref[...]), tab_ref, T_QPE, MLA_ROPE // 2)
    for h in range(HEADS):
        qa_ref[h, :, 0:MLA_KV_RANK] = q_lat[:, h * LANES:(h + 1) * LANES].astype(BF)
        qa_ref[h, :, MLA_KV_RANK:] = q_pe[:, h * MLA_ROPE:(h + 1) * MLA_ROPE].astype(BF)
    ckv = proj(P_CKV, P_MISC)
    c_new = ckv * lax.rsqrt(jnp.mean(ckv * ckv, -1, keepdims=True) + EPS) * kvn_ref[...]
    misc = proj(P_MISC, P_Z)
    misc_r = _rope(misc, tab_ref, T_MISC, MLA_ROPE // 2)
    mla_ref[:, 0:MLA_KV_RANK] = c_new
    mla_ref[:, MLA_KV_RANK:] = misc_r[:, 0:MLA_ROPE]
    mlab_ref[:, 0:MLA_KV_RANK] = c_new.astype(BF)
    mlab_ref[:, MLA_KV_RANK:] = misc_r[:, 0:MLA_ROPE].astype(BF)
    is_gate = (lane >= M_GATE) & (lane < M_GATE + 3 * HEADS)
    misc_ref[...] = jnp.where(is_gate, 1.0 / (1.0 + jnp.exp(-misc)), misc_r)
    z_ref[...] = proj(P_Z, P_DQ)

    dq = _rope(proj(P_DQ, P_DIFFKV), tab_ref, T_H32, DIFF_DK // 8)
    for h in range(HEADS):
        slab = dq[:, (h // 2) * LANES:(h // 2 + 1) * LANES]
        if h % 2:
            slab = pltpu.roll(slab, HEAD_DIM, 1)
        dq_ref[h] = jnp.where(lane < DIFF_DK, slab, 0.0).astype(BF)
        dq_ref[HEADS + h] = jnp.where((lane >= DIFF_DK) & (lane < 2 * DIFF_DK), slab, 0.0).astype(BF)
    diff_row = _rope(proj(P_DIFFKV, P_NQ), tab_ref, T_DIFFKV, DIFF_DK // 8)
    diff_ref[...] = diff_row
    diffb_ref[...] = diff_row.astype(BF)

    def per_head(v, out_ref):
        for h in range(HEADS):
            slab = v[:, (h // 2) * LANES:(h // 2 + 1) * LANES]
            if h % 2:
                slab = pltpu.roll(slab, HEAD_DIM, 1)
            out_ref[h] = jnp.where(lane < HEAD_DIM, slab, 0.0).astype(BF)

    nq = proj(P_NQ, P_CMP)
    per_head(nq, nq_ref)
    per_head(_rope(nq, tab_ref, T_H64, NSA_DK // 8), nqr_ref)
    cmp_ref[...] = proj(P_CMP, P_SEL)
    sel_row = _rope(proj(P_SEL, P_WIN), tab_ref, T_KV64, NSA_DK // 8)
    sel_ref[...] = sel_row
    selb_ref[...] = sel_row.astype(BF)
    win_row = _rope(proj(P_WIN, P_SQ), tab_ref, T_KV64, NSA_DK // 8)
    win_ref[...] = win_row
    winb_ref[...] = win_row.astype(BF)

    per_head(_rope(proj(P_SQ, P_DSAKV), tab_ref, T_H64, DSA_DK // 8), sq_ref)
    kv_row = _rope(proj(P_DSAKV, P_IQ), tab_ref, T_KV64, DSA_DK // 8)
    kv_ref[...] = kv_row
    kvb_ref[...] = kv_row.astype(BF)
    iq = _rope(proj(P_IQ, P_IK4), tab_ref, T_H32, DSA_IDX_DIM // 8)
    for h in range(HEADS):
        live = (lane >= h * DSA_IDX_DIM) & (lane < (h + 1) * DSA_IDX_DIM)
        iq_ref[h] = jnp.where(live, iq, 0.0).astype(BF)
    ik4 = _rope(proj(P_IK4, P_END), tab_ref, T_H32, DSA_IDX_DIM // 8)
    idx_ref[...] = ik4[:, 0:DSA_IDX_DIM]
    ik4b_ref[...] = ik4.astype(BF)


def _inproj(x3, mod3, g, wr, qn, kvn, wuqn, wuqp, wukbd, tab, tm):
    bk, sk, d = x3.shape
    per_token = mod3.shape[1] != 1
    grid = (sk // tm, bk)
    tok = lambda s, b: (b, s, 0)
    hed = lambda s, b: (0, b, s, 0)
    in_specs = [
        pl.BlockSpec((None, tm, d), tok),
        pl.BlockSpec((None, tm, 3 * d), tok) if per_token else pl.BlockSpec((None, 1, 3 * d), lambda s, b: (b, 0, 0)),
        _const_spec(g), _const_spec(wr), _const_spec(qn), _const_spec(kvn),
        _const_spec(wuqn), _const_spec(wuqp), _const_spec(wukbd),
        pl.BlockSpec((3, tm, T_END), lambda s, b: (0, s, 0)),
    ]

    def hs(n, w):
        return jax.ShapeDtypeStruct((n, bk, sk, w), BF), pl.BlockSpec((n, None, tm, w), hed)

    def ts(w, dt):
        return jax.ShapeDtypeStruct((bk, sk, w), dt), pl.BlockSpec((None, tm, w), tok)

    outs = [hs(HEADS, MLA_KV_RANK + MLA_ROPE), hs(2 * HEADS, LANES), hs(HEADS, LANES), hs(HEADS, LANES),
            hs(HEADS, LANES), hs(HEADS, LANES), ts(LANES, F32), ts(4 * GROUP_WIDTH, F32),
            ts(MLA_KV_RANK + MLA_ROPE, F32), ts(LANES, F32), ts(LANES, F32), ts(LANES, F32), ts(LANES, F32),
            ts(LANES, F32), ts(DSA_IDX_DIM, F32),
            ts(MLA_KV_RANK + MLA_ROPE, BF), ts(LANES, BF), ts(LANES, BF), ts(LANES, BF), ts(LANES, BF),
            ts(LANES, BF)]
    names = ("qa", "dq", "nq", "nqr", "sq", "iq", "misc", "z", "mla", "diff", "cmp", "sel", "win", "kv", "idx",
             "mla_b", "diff_b", "sel_b", "win_b", "kv_b", "ik4_b")
    res = pl.pallas_call(
        _inproj_body,
        out_shape=[o[0] for o in outs],
        grid=grid,
        in_specs=in_specs,
        out_specs=[o[1] for o in outs],
        compiler_params=_params(("arbitrary", "arbitrary")),
        name="in_proj",
    )(x3, mod3, g, wr, qn, kvn, wuqn, wuqp, wukbd, tab)
    return dict(zip(names, res))


def _chunk_proj(row_refs, w1s_ref):
    acc = None
    for s in range(NSA_CMP_STRIDE):
        parts = [r[pl.ds(s, r.shape[0] // NSA_CMP_STRIDE, stride=NSA_CMP_STRIDE), :] for r in row_refs]
        xs = parts[0] if len(parts) == 1 else jnp.concatenate(parts, axis=0)
        t = _dot(xs.astype(BF), w1s_ref[s])
        acc = t if acc is None else acc + t
    return acc


def _cmp1_prompt_body(rows_ref, w1s_ref, u_ref):
    u_ref[...] = _chunk_proj([rows_ref], w1s_ref)


def _cmp1_prompt(cmp_rows, w1s):
    b, s, _ = cmp_rows.shape
    return pl.pallas_call(
        _cmp1_prompt_body,
        out_shape=jax.ShapeDtypeStruct((b, s // NSA_CMP_STRIDE, 2 * LANES), F32),
        grid=(b,),
        in_specs=[pl.BlockSpec((None, s, LANES), lambda i: (i, 0, 0)), _const_spec(w1s)],
        out_specs=pl.BlockSpec((None, s // NSA_CMP_STRIDE, 2 * LANES), lambda i: (i, 0, 0)),
        compiler_params=_params(("arbitrary",)),
        name="nsa_cmp1_prompt",
    )(cmp_rows, w1s)


def _page_specs(cache, layer, n_pages, group, n_extra):
    page, w = cache.shape[2], cache.shape[3]

    def make(k):
        def imap(b, g, pt):
            return (layer, pt[b * n_pages + g * group + k], 0, 0)
        return pl.BlockSpec((None, None, page, w), imap)
    return [make(k) for k in range(group)]


def _cmp1_sample_body(group, pt_ref, *refs):
    pages, w1s_ref, u_ref = refs[:group], refs[group], refs[group + 1]
    u_ref[...] = _chunk_proj(list(pages), w1s_ref)


def _cmp1_sample(cache, layer, pt_flat, bd, n_pages, w1s, group):
    page = cache.shape[2]
    cpp = page // NSA_CMP_STRIDE
    gs = pltpu.PrefetchScalarGridSpec(
        num_scalar_prefetch=1,
        grid=(bd, n_pages // group),
        in_specs=_page_specs(cache, layer, n_pages, group, 0) + [pl.BlockSpec(w1s.shape, lambda b, g, pt: (0, 0, 0))],
        out_specs=pl.BlockSpec((None, group * cpp, 2 * LANES), lambda b, g, pt: (b, g, 0)),
    )
    return pl.pallas_call(
        functools.partial(_cmp1_sample_body, group),
        out_shape=jax.ShapeDtypeStruct((bd, n_pages * cpp, 2 * LANES), F32),
        grid_spec=gs,
        compiler_params=_params(("arbitrary", "arbitrary")),
        name="nsa_cmp1_sample",
    )(pt_flat, *([cache] * group), w1s)


def _cmp2_body(u_ref, pos_ref, w1s_ref, w2_ref, o_ref):
    n = u_ref.shape[0]
    pacc = None
    for s in range(NSA_CMP_STRIDE):
        t = _dot(pos_ref[s].astype(BF), w1s_ref[s])
        pacc = t if pacc is None else pacc + t
    posterm = pacc[0:1, 0:LANES] + pacc[1:2, LANES:2 * LANES]
    u = u_ref[...]
    pre = u[:, 0:LANES] + pltpu.roll(u[:, LANES:2 * LANES], n - 1, 0) + posterm
    o_ref[...] = _dot(_silu(pre).astype(BF), w2_ref[...])


def _cmp2(u, pos8, w1s, w2bd):
    b, n, _ = u.shape
    return pl.pallas_call(
        _cmp2_body,
        out_shape=jax.ShapeDtypeStruct((b, n, LANES), F32),
        grid=(b,),
        in_specs=[pl.BlockSpec((None, n, 2 * LANES), lambda i: (i, 0, 0)),
                  _const_spec(pos8), _const_spec(w1s), _const_spec(w2bd)],
        out_specs=pl.BlockSpec((None, n, LANES), lambda i: (i, 0, 0)),
        compiler_params=_params(("arbitrary",)),
        name="nsa_cmp2",
    )(u, pos8, w1s, w2bd)


def _softmax_parts(s, mask):
    s = jnp.where(mask, s, NEG)
    m = jnp.max(s, -1, keepdims=True)
    p = jnp.where(mask, jnp.exp(s - m), 0.0)
    return p, 1.0 / jnp.maximum(jnp.sum(p, -1, keepdims=True), 1e-30)


def _lambda(lam_ref, lam_init):
    lp = lam_ref[...]
    a = jnp.sum(lp[0:1] * lp[1:2], -1, keepdims=True)
    b = jnp.sum(lp[2:3] * lp[3:4], -1, keepdims=True)
    return jnp.exp(a) - jnp.exp(b) + lam_init


def _sort_key(x):
    bits = lax.bitcast_convert_type(x, jnp.int32)
    return jnp.where(bits < 0, bits ^ jnp.int32(0x7FFFFFFF), bits)


def _kth_largest_key(key, k, count):
    rows = count(key >= 0)
    t = jnp.where(rows >= k, jnp.int32(0), jnp.int32(INT_MIN))

    def step(i, t):
        cand = t + jnp.left_shift(jnp.int32(1), 30 - i)
        return jnp.where(count(key >= cand) >= k, cand, t)
    return lax.fori_loop(0, 31, step, t)


def _tri_consts():
    r = lax.broadcasted_iota(jnp.int32, (LANES, LANES), 0)
    c = lax.broadcasted_iota(jnp.int32, (LANES, LANES), 1)
    return jnp.where(r < c, 1.0, 0.0).astype(BF), jnp.ones((LANES, LANES), BF)


def _place_heads(parts, lane):
    slabs = []
    for m in range(HEADS // 2):
        slabs.append(jnp.where(lane < HEAD_DIM, pltpu.roll(parts[2 * m], HEAD_DIM, 1), parts[2 * m + 1]))
    return jnp.concatenate(slabs, axis=1)


def _prompt_attn_body(n_cmp, n_sel, n_top, k_sel, lam_init,
                      qa_ref, dq_ref, nq_ref, nqr_ref, sq_ref, iq_ref, misc_ref,
                      mla_ref, diff_ref, cmpkv_ref, sel_ref, win_ref, kv_ref, ik4_ref,
                      wuv_ref, subln_ref, lam_ref, ovt_ref, e_ref, o_ref):
    tq = misc_ref.shape[0]
    s_len = mla_ref.shape[0]
    nsp = ovt_ref.shape[0]
    q0 = pl.program_id(1) * tq
    qpos = q0 + lax.broadcasted_iota(jnp.int32, (tq, 1), 0)
    kpos = lax.broadcasted_iota(jnp.int32, (1, s_len), 1)
    causal = kpos <= qpos
    lane = lax.broadcasted_iota(jnp.int32, (tq, LANES), 1)
    misc = misc_ref[...]
    lam = _lambda(lam_ref, lam_init)

    kc = mla_ref[:, 0:MLA_KV_RANK]
    o_a = None
    for h in range(HEADS):
        p, r = _softmax_parts(_dot_t(qa_ref[h], mla_ref[...]) * MLA_SCALE, causal)
        o_lat = _dot(p.astype(BF), kc) * r
        t = _dot(o_lat.astype(BF), wuv_ref[h])
        o_a = t if o_a is None else o_a + t
    o_ref[:, 0:GROUP_WIDTH] = o_a

    kd = diff_ref[...]
    parts = []
    for h in range(HEADS):
        p1, r1 = _softmax_parts(_dot_t(dq_ref[h], kd) * DIFF_DK ** -0.5, causal)
        p2, r2 = _softmax_parts(_dot_t(dq_ref[HEADS + h], kd) * DIFF_DK ** -0.5, causal)
        o = _dot(p1.astype(BF), kd) * r1 - lam * (_dot(p2.astype(BF), kd) * r2)
        ms = jnp.sum(jnp.where(lane >= HEAD_DIM, o * o, 0.0), -1, keepdims=True) * (1.0 / HEAD_DIM)
        parts.append(o * lax.rsqrt(ms + EPS) * subln_ref[...] * (1.0 - lam_init))
    o_ref[:, GROUP_WIDTH:2 * GROUP_WIDTH] = _place_heads(parts, lane)

    ckv = cmpkv_ref[...].astype(BF)
    ncol = lax.broadcasted_iota(jnp.int32, (1, ckv.shape[0]), 1)
    cmask = (ncol * NSA_CMP_STRIDE + (NSA_CMP_BLOCK - 1) <= qpos) & (ncol < n_cmp)
    o_cmp, pcsum = [], None
    for h in range(HEADS):
        p, r = _softmax_parts(_dot_t(nq_ref[h], ckv) * NSA_DK ** -0.5, cmask)
        pc = p * r
        o_cmp.append(_dot(pc.astype(BF), ckv))
        pcsum = pc if pcsum is None else pcsum + pc
    hi = pcsum.astype(BF)
    lo = (pcsum - hi.astype(F32)).astype(BF)
    imp_t = _dot_t(ovt_ref[...], hi) + _dot_t(ovt_ref[...], lo)
    jb = lax.broadcasted_iota(jnp.int32, (nsp, tq), 0)
    qblk = (q0 + lax.broadcasted_iota(jnp.int32, (1, tq), 1)) // NSA_SEL_BLOCK
    forced = (jb == 0) | (jb > qblk - NSA_LOCAL)
    score = jnp.where(jb > qblk, NEG, jnp.where(forced, BIG, imp_t))
    score = jnp.where(jb < n_sel, score, -jnp.inf)
    sel_t = jnp.zeros((nsp, tq), F32)
    for _ in range(n_top):
        m = jnp.max(score, 0, keepdims=True)
        first = jnp.min(jnp.where(score == m, jb, nsp), 0, keepdims=True)
        hit = jb == first
        sel_t = jnp.where(hit, 1.0, sel_t)
        score = jnp.where(hit, -jnp.inf, score)
    sel_keys = lax.dot_general(sel_t.astype(BF), e_ref[...], (((0,), (0,)), ((), ())),
                               preferred_element_type=F32)
    smask = (sel_keys > 0.5) & causal
    ks = sel_ref[...]
    o_sel = []
    for h in range(HEADS):
        p, r = _softmax_parts(_dot_t(nqr_ref[h], ks) * NSA_DK ** -0.5, smask)
        o_sel.append(_dot(p.astype(BF), ks) * r)
    span = NSA_WINDOW + tq
    start = pl.multiple_of(jnp.maximum(q0 - NSA_WINDOW, 0), tq)
    kw = win_ref[pl.ds(start, span), :]
    wpos = start + lax.broadcasted_iota(jnp.int32, (1, span), 1)
    wmask = (wpos <= qpos) & (wpos > qpos - NSA_WINDOW)
    parts = []
    for h in range(HEADS):
        p, r = _softmax_parts(_dot_t(nqr_ref[h], kw) * NSA_DK ** -0.5, wmask)
        o_win = _dot(p.astype(BF), kw) * r
        g = [misc[:, M_GATE + HEADS * i + h:M_GATE + HEADS * i + h + 1] for i in range(3)]
        parts.append(g[0] * o_cmp[h] + g[1] * o_sel[h] + g[2] * o_win)
    o_ref[:, 2 * GROUP_WIDTH:3 * GROUP_WIDTH] = _place_heads(parts, lane)

    ki = ik4_ref[...]
    isc = None
    for h in range(HEADS):
        t = jnp.maximum(_dot_t(iq_ref[h], ki), 0.0) * misc[:, M_IW + h:M_IW + h + 1]
        isc = t if isc is None else isc + t
    isc = jnp.where(causal, isc * DSA_IDX_SCALE + 0.0, NEG)
    key = _sort_key(isc)
    count = lambda pred: jnp.sum(jnp.where(pred, 1.0, 0.0), -1, keepdims=True)
    kth = _kth_largest_key(key, float(k_sel), count)
    need = float(k_sel) - count(key > kth)
    tri, ones = _tri_consts()
    offs = jnp.zeros((tq, LANES), F32)
    chunks = []
    for c in range(s_len // LANES):
        kc_ = key[:, c * LANES:(c + 1) * LANES]
        eq = jnp.where(kc_ == kth, 1.0, 0.0).astype(BF)
        before = _dot(eq, tri) + offs
        offs = offs + _dot(eq, ones)
        chunks.append(jnp.where((kc_ > kth) | ((kc_ == kth) & (before < need)), 1.0, 0.0))
    dmask = (jnp.concatenate(chunks, axis=1) > 0.5) & causal
    kv = kv_ref[...]
    parts = []
    for h in range(HEADS):
        p, r = _softmax_parts(_dot_t(sq_ref[h], kv) * DSA_DK ** -0.5, dmask)
        parts.append(_dot(p.astype(BF), kv) * r)
    o_ref[:, 3 * GROUP_WIDTH:] = _place_heads(parts, lane)


def _prompt_attn(pr, cmpkv, wuv, subln, lam_p, ovt, emat, lam_init, n_cmp, n_sel, n_top, k_sel, tq):
    _, b, s, _ = pr["qa"].shape
    hed = lambda i, j: (0, i, j, 0)
    seq = lambda i, j: (i, 0, 0)

    def hs(a):
        return pl.BlockSpec((a.shape[0], None, tq, a.shape[3]), hed)

    def ks(a):
        return pl.BlockSpec((None,) + a.shape[1:], seq)
    qnames = ("qa", "dq", "nq", "nqr", "sq", "iq")
    knames = ("mla_b", "diff_b", None, "sel_b", "win_b", "kv_b", "ik4_b")
    kargs = [cmpkv if n is None else pr[n] for n in knames]
    consts = [wuv, subln, lam_p, ovt, emat]
    return pl.pallas_call(
        functools.partial(_prompt_attn_body, n_cmp, n_sel, n_top, k_sel, lam_init),
        out_shape=jax.ShapeDtypeStruct((b, s, 4 * GROUP_WIDTH), F32),
        grid=(b, s // tq),
        in_specs=[hs(pr[n]) for n in qnames] + [pl.BlockSpec((None, tq, LANES), lambda i, j: (i, j, 0))]
        + [ks(a) for a in kargs] + [_const_spec(c) for c in consts],
        out_specs=pl.BlockSpec((None, tq, 4 * GROUP_WIDTH), lambda i, j: (i, j, 0)),
        compiler_params=_params(("arbitrary", "arbitrary")),
        name="prompt_mixers",
    )(*[pr[n] for n in qnames], pr["misc"], *kargs, *consts)


def _sample_select_body(group, n_pages, n_cmp, n_sel, n_top, k_sel, qpos, pt_ref, *refs):
    pages = refs[:group]
    (iq_ref, iw_ref, ikn_ref, nq_ref, cmpkv_ref, ov_ref,
     dmask_ref, smask_ref, ocmp_ref, isc_ref) = refs[group:]
    g = pl.program_id(1)
    page = pages[0].shape[0]
    rp = isc_ref.shape[0]
    q = iq_ref[...]
    w = iw_ref[...]
    sub = lax.broadcasted_iota(jnp.int32, (group, page), 0)
    rows = jnp.zeros((group, page), F32)
    for k in range(group):
        s = jnp.maximum(_dot_t(q, pages[k][...].astype(BF)), 0.0) * w
        rows = jnp.where(sub == k, jnp.sum(s, 0, keepdims=True) * DSA_IDX_SCALE + 0.0, rows)
    isc_ref[pl.ds(pl.multiple_of(g * group, group), group), :] = rows

    @pl.when(g == pl.num_programs(1) - 1)
    def _():
        subt = lax.broadcasted_iota(jnp.int32, (rp - n_pages, page), 0)
        lanet = lax.broadcasted_iota(jnp.int32, (rp - n_pages, page), 1)
        kn = ikn_ref[...].astype(BF).astype(F32)
        sn = jnp.maximum(jnp.sum(q.astype(F32) * kn, -1, keepdims=True), 0.0) * w
        rn = jnp.sum(sn, 0, keepdims=True) * DSA_IDX_SCALE + 0.0
        isc_ref[n_pages:rp, :] = jnp.where((subt == 0) & (lanet == 0), rn, -jnp.inf)
        key = _sort_key(isc_ref[...])
        count = lambda pred: jnp.sum(jnp.sum(jnp.where(pred, 1.0, 0.0), 1, keepdims=True), 0, keepdims=True)
        kth = _kth_largest_key(key, float(k_sel), count)
        need = float(k_sel) - count(key > kth)
        tri, ones = _tri_consts()
        eq = jnp.where(key == kth, 1.0, 0.0).astype(BF)
        rr = lax.broadcasted_iota(jnp.int32, (rp, rp), 0)
        cc = lax.broadcasted_iota(jnp.int32, (rp, rp), 1)
        rows_before = _dot(jnp.where(cc < rr, 1.0, 0.0).astype(BF), _dot(eq, ones).astype(BF))
        before = _dot(eq, tri) + rows_before
        dmask_ref[...] = jnp.where((key > kth) | ((key == kth) & (before < need)), 1.0, 0.0)
        ckv = cmpkv_ref[...].astype(BF)
        ncol = lax.broadcasted_iota(jnp.int32, (1, ckv.shape[0]), 1)
        cmask = (ncol * NSA_CMP_STRIDE + (NSA_CMP_BLOCK - 1) <= qpos) & (ncol < n_cmp)
        p, r = _softmax_parts(_dot_t(nq_ref[...], ckv) * NSA_DK ** -0.5, cmask)
        pc = p * r
        ocmp_ref[...] = _dot(pc.astype(BF), ckv)
        head_rows = lax.broadcasted_iota(jnp.int32, pc.shape, 0) < HEADS
        pcs = jnp.where(head_rows, pc, 0.0)
        hi = pcs.astype(BF)
        lo = (pcs - hi.astype(F32)).astype(BF)
        imp = jnp.sum(_dot(hi, ov_ref[...]) + _dot(lo, ov_ref[...]), 0, keepdims=True)
        nsl = imp.shape[1]
        jb = lax.broadcasted_iota(jnp.int32, (1, nsl), 1)
        qblk = qpos // NSA_SEL_BLOCK
        forced = (jb == 0) | (jb > qblk - NSA_LOCAL)
        score = jnp.where(jb > qblk, NEG, jnp.where(forced, BIG, imp))
        score = jnp.where(jb < n_sel, score, -jnp.inf)
        sel = jnp.zeros((1, nsl), F32)
        for _ in range(n_top):
            m = jnp.max(score, 1, keepdims=True)
            first = jnp.min(jnp.where(score == m, jb, nsl), 1, keepdims=True)
            hit = jb == first
            sel = jnp.where(hit, 1.0, sel)
            score = jnp.where(hit, -jnp.inf, score)
        bpp = page // NSA_SEL_BLOCK
        pr_ = lax.broadcasted_iota(jnp.int32, (rp, nsl), 0)
        pj = lax.broadcasted_iota(jnp.int32, (rp, nsl), 1)
        a = jnp.where(pj // bpp == pr_, sel, 0.0).astype(BF)
        fj = lax.broadcasted_iota(jnp.int32, (nsl, page), 0)
        fc = lax.broadcasted_iota(jnp.int32, (nsl, page), 1)
        f = jnp.where(fj % bpp == fc // NSA_SEL_BLOCK, 1.0, 0.0).astype(BF)
        smask_ref[...] = _dot(a, f)


def _sample_select(cache_idx, layer, pt_flat, bd, n_pages, iq8, iw8, ik_new, nq8, cmpkv, ov,
                   n_cmp, n_sel, n_top, k_sel, qpos, group):
    page = cache_idx.shape[2]
    rp = _round_up(n_pages + 1, LANES)
    per_b = lambda b, g, pt: (b, 0, 0)

    def bs(a):
        return pl.BlockSpec((None,) + a.shape[1:], per_b)
    gs = pltpu.PrefetchScalarGridSpec(
        num_scalar_prefetch=1,
        grid=(bd, n_pages // group),
        in_specs=_page_specs(cache_idx, layer, n_pages, group, 0)
        + [bs(iq8), bs(iw8), bs(ik_new), bs(nq8), bs(cmpkv), pl.BlockSpec(ov.shape, lambda b, g, pt: (0, 0))],
        out_specs=[pl.BlockSpec((None, rp, page), per_b), pl.BlockSpec((None, rp, page), per_b),
                   pl.BlockSpec((None, 8, LANES), per_b)],
        scratch_shapes=[pltpu.VMEM((rp, page), F32)],
    )
    return pl.pallas_call(
        functools.partial(_sample_select_body, group, n_pages, n_cmp, n_sel, n_top, k_sel, qpos),
        out_shape=[jax.ShapeDtypeStruct((bd, rp, page), F32), jax.ShapeDtypeStruct((bd, rp, page), F32),
                   jax.ShapeDtypeStruct((bd, 8, LANES), F32)],
        grid_spec=gs,
        compiler_params=_params(("arbitrary", "arbitrary")),
        name="sample_select",
    )(pt_flat, *([cache_idx] * group), iq8, iw8, ik_new, nq8, cmpkv, ov)


def _online_update(q, pages, mask_rows, scale, m_ref, l_ref, acc_ref):
    s = jnp.concatenate([_dot_t(q, pg) for pg in pages], axis=1) * scale
    if mask_rows is not None:
        mk = jnp.concatenate(mask_rows, axis=1) > 0.5
        s = jnp.where(mk, s, NEG)
    m_old = m_ref[...]
    m_new = jnp.maximum(m_old, jnp.max(s, -1, keepdims=True))
    alpha = jnp.exp(m_old - m_new)
    p = jnp.exp(s - m_new)
    if mask_rows is not None:
        p = jnp.where(mk, p, 0.0)
    l_ref[...] = alpha * l_ref[...] + jnp.sum(p, -1, keepdims=True)
    n = pages[0].shape[0]
    pv = None
    for k, pg in enumerate(pages):
        t = _dot(p[:, k * n:(k + 1) * n].astype(BF), pg)
        pv = t if pv is None else pv + t
    acc_ref[...] = alpha * acc_ref[...] + pv
    m_ref[...] = m_new


def _merge_new_key(q, row, valid, scale, m, l, acc):
    rb = row.astype(BF).astype(F32)
    s = jnp.sum(q.astype(F32) * rb, -1, keepdims=True) * scale
    s = jnp.where(valid, s, NEG)
    m_new = jnp.maximum(m, s)
    alpha = jnp.exp(m - m_new)
    p = jnp.where(valid, jnp.exp(s - m_new), 0.0)
    l_new = alpha * l + p
    acc_new = alpha * acc + p.astype(BF).astype(F32) * rb
    return acc_new / jnp.maximum(l_new, 1e-30)


def _sample_attn_body(group, n_pages, lam_init, qpos, pt_ref, *refs):
    mla_pg, diff_pg = refs[0:group], refs[group:2 * group]
    sel_pg, kv_pg = refs[2 * group:3 * group], refs[3 * group:4 * group]
    (qa_ref, dq_ref, nqr_ref, sq_ref, misc_ref, ocmp_ref, smask_ref, dmask_ref,
     mlan_ref, diffn_ref, seln_ref, winn_ref, kvn_ref, wbuf_ref,
     wuv_ref, subln_ref, lam_ref, o_ref,
     ma, la, acca, mb, lb, accb, mc, lc, accc, md, ld, accd) = refs[4 * group:]
    g = pl.program_id(1)
    states = ((ma, la, acca), (mb, lb, accb), (mc, lc, accc), (md, ld, accd))

    @pl.when(g == 0)
    def _():
        for m_ref, l_ref, acc_ref in states:
            m_ref[...] = jnp.full(m_ref.shape, NEG, F32)
            l_ref[...] = jnp.zeros(l_ref.shape, F32)
            acc_ref[...] = jnp.zeros(acc_ref.shape, F32)

    def bf_pages(pgs):
        return [p[...].astype(BF) for p in pgs]

    def mask_rows(mref):
        return [mref[pl.ds(g * group + k, 1), :] for k in range(group)]

    _online_update(qa_ref[...], bf_pages(mla_pg), None, MLA_SCALE, ma, la, acca)
    _online_update(dq_ref[...], bf_pages(diff_pg), None, DIFF_DK ** -0.5, mb, lb, accb)
    _online_update(nqr_ref[...], bf_pages(sel_pg), mask_rows(smask_ref), NSA_DK ** -0.5, mc, lc, accc)
    _online_update(sq_ref[...], bf_pages(kv_pg), mask_rows(dmask_ref), DSA_DK ** -0.5, md, ld, accd)

    @pl.when(g == pl.num_programs(1) - 1)
    def _():
        lane = lax.broadcasted_iota(jnp.int32, (8, LANES), 1)
        rowi = lax.broadcasted_iota(jnp.int32, (8, LANES), 0)
        misc = misc_ref[...]
        lam = _lambda(lam_ref, lam_init)
        true8 = jnp.full((8, 1), True)
        oa = _merge_new_key(qa_ref[...], mlan_ref[...], true8, MLA_SCALE, ma[...], la[...], acca[...])
        o_lat = oa[:, 0:MLA_KV_RANK]
        o_a = None
        for h in range(HEADS):
            t = _dot(jnp.where(rowi == h, o_lat, 0.0).astype(BF), wuv_ref[h])
            o_a = t if o_a is None else o_a + t
        o_ref[:, 0:GROUP_WIDTH] = jnp.sum(o_a, 0, keepdims=True)
        ob = _merge_new_key(dq_ref[...], diffn_ref[...], true8, DIFF_DK ** -0.5, mb[...], lb[...], accb[...])
        o = ob - lam * pltpu.roll(ob, HEADS, 0)
        ms = jnp.sum(jnp.where(lane >= HEAD_DIM, o * o, 0.0), -1, keepdims=True) * (1.0 / HEAD_DIM)
        o = o * lax.rsqrt(ms + EPS) * subln_ref[...] * (1.0 - lam_init)
        o_ref[:, GROUP_WIDTH:2 * GROUP_WIDTH] = _place_heads([o[h:h + 1] for h in range(HEADS)], lane[0:1])
        snew = smask_ref[n_pages:n_pages + 1, 0:1] > 0.5
        osel = _merge_new_key(nqr_ref[...], seln_ref[...], snew, NSA_DK ** -0.5, mc[...], lc[...], accc[...])
        wb = wbuf_ref[...].astype(BF)
        nwin = wb.shape[0]
        wpos = qpos - nwin + lax.broadcasted_iota(jnp.int32, (1, nwin), 1)
        wmask = (wpos <= qpos) & (wpos > qpos - NSA_WINDOW) & (wpos >= 0)
        sw = jnp.where(wmask, _dot_t(nqr_ref[...], wb) * NSA_DK ** -0.5, NEG)
        mw = jnp.max(sw, -1, keepdims=True)
        pw = jnp.where(wmask, jnp.exp(sw - mw), 0.0)
        owin = _merge_new_key(nqr_ref[...], winn_ref[...], true8, NSA_DK ** -0.5, mw,
                              jnp.sum(pw, -1, keepdims=True), _dot(pw.astype(BF), wb))
        ocmp = ocmp_ref[...]
        parts = []
        for h in range(HEADS):
            gt = [misc[:, M_GATE + HEADS * i + h:M_GATE + HEADS * i + h + 1] for i in range(3)]
            parts.append(gt[0] * ocmp[h:h + 1] + gt[1] * osel[h:h + 1] + gt[2] * owin[h:h + 1])
        o_ref[:, 2 * GROUP_WIDTH:3 * GROUP_WIDTH] = _place_heads(parts, lane[0:1])
        dnew = dmask_ref[n_pages:n_pages + 1, 0:1] > 0.5
        od = _merge_new_key(sq_ref[...], kvn_ref[...], dnew, DSA_DK ** -0.5, md[...], ld[...], accd[...])
        o_ref[:, 3 * GROUP_WIDTH:] = _place_heads([od[h:h + 1] for h in range(HEADS)], lane[0:1])


def _sample_attn(caches, layer, pt_flat, bd, n_pages, per_b_args, consts, lam_init, qpos, group):
    per_b = lambda b, g, pt: (b, 0, 0)

    def bs(a):
        return pl.BlockSpec((None,) + a.shape[1:], per_b)

    def cs(a):
        nd = a.ndim
        return pl.BlockSpec(a.shape, lambda b, g, pt: (0,) * nd)
    page_specs, page_args = [], []
    for c in caches:
        page_specs += _page_specs(c, layer, n_pages, group, 0)
        page_args += [c] * group
    wm = caches[0].shape[3]
    scratch = []
    for w in (wm, LANES, LANES, LANES):
        scratch += [pltpu.VMEM((8, 1), F32), pltpu.VMEM((8, 1), F32), pltpu.VMEM((8, w), F32)]
    gs = pltpu.PrefetchScalarGridSpec(
        num_scalar_prefetch=1,
        grid=(bd, n_pages // group),
        in_specs=page_specs + [bs(a) for a in per_b_args] + [cs(a) for a in consts],
        out_specs=pl.BlockSpec((None, 1, 4 * GROUP_WIDTH), per_b),
        scratch_shapes=scratch,
    )
    return pl.pallas_call(
        functools.partial(_sample_attn_body, group, n_pages, lam_init, qpos),
        out_shape=jax.ShapeDtypeStruct((bd, 1, 4 * GROUP_WIDTH), F32),
        grid_spec=gs,
        compiler_params=_params(("arbitrary", "arbitrary")),
        name="sample_mixers",
    )(pt_flat, *page_args, *per_b_args, *consts)


def _outproj_body(final, o_ref, z_ref, x_ref, mod_ref, w_ref, fn_ref, out_ref):
    d = x_ref.shape[-1]
    y = _dot((o_ref[...] * _silu(z_ref[...])).astype(BF), w_ref[...])
    xn = x_ref[...] + mod_ref[...][:, 2 * d:3 * d] * y
    if final:
        xn = xn * lax.rsqrt(jnp.mean(xn * xn, -1, keepdims=True) + EPS) * fn_ref[...]
    out_ref[...] = xn


def _outproj(o3, z3, x3, mod3, wout, fn, final, tm):
    bk, sk, d = x3.shape
    per_token = mod3.shape[1] != 1
    tok = lambda s, b: (b, s, 0)
    return pl.pallas_call(
        functools.partial(_outproj_body, final),
        out_shape=jax.ShapeDtypeStruct((bk, sk, d), F32),
        grid=(sk // tm, bk),
        in_specs=[pl.BlockSpec((None, tm, o3.shape[2]), tok), pl.BlockSpec((None, tm, z3.shape[2]), tok),
                  pl.BlockSpec((None, tm, d), tok),
                  pl.BlockSpec((None, tm, 3 * d), tok) if per_token
                  else pl.BlockSpec((None, 1, 3 * d), lambda s, b: (b, 0, 0)),
                  _const_spec(wout), _const_spec(fn)],
        out_specs=pl.BlockSpec((None, tm, d), tok),
        compiler_params=_params(("arbitrary", "arbitrary")),
        name="out_proj",
    )(o3, z3, x3, mod3, wout, fn)


def _in_columns():
    widths = (192, 128, 32, 256, 256, 64, 64, 256, 256, 128, 128, 128, 12, 256, 256, 128, 128, 4, 32, 256)
    names = ("mla_cq", "mla_ckv", "mla_kr", "mla_z", "diff_q", "diff_k", "diff_v", "diff_z", "nsa_q", "nsa_cmp",
             "nsa_sel", "nsa_win", "nsa_gate", "nsa_z", "dsa_q", "dsa_kv", "dsa_iq", "dsa_iw", "dsa_ik", "dsa_z")
    off, o = {}, 0
    for n, w in zip(names, widths):
        off[n] = (o, w)
        o += w

    def rng(n):
        return list(range(off[n][0], off[n][0] + off[n][1]))
    pad = lambda k: [-1] * k
    cols = (rng("mla_cq") + pad(64) + rng("mla_ckv")
            + rng("mla_kr") + rng("nsa_gate") + rng("dsa_iw") + pad(LANES - 48)
            + rng("mla_z") + rng("diff_z") + rng("nsa_z") + rng("dsa_z")
            + rng("diff_q") + rng("diff_k") + rng("diff_v") + rng("nsa_q") + rng("nsa_cmp") + rng("nsa_sel")
            + rng("nsa_win") + rng("dsa_q") + rng("dsa_kv") + rng("dsa_iq") + rng("dsa_ik") * 4)
    cols = np.asarray(cols, np.int32)
    assert cols.shape[0] == P_END and o == 2960
    return cols


def _rope_pattern(pos, width, period, rot, active):
    half = rot // 2
    j = np.arange(width)
    jj = j % period
    live = (j < active) & (jj < rot)
    inv = ROPE_THETA ** (-jnp.arange(half, dtype=F32) * 2.0 / rot)
    ang = pos.astype(F32)[:, None] * inv
    cos, sin = jnp.cos(ang), jnp.sin(ang)
    idx = jnp.asarray(jj % half)
    c = jnp.where(jnp.asarray(live), cos[:, idx], 1.0)
    sa = jnp.where(jnp.asarray(live & (jj < half)), -sin[:, idx], 0.0)
    sb = jnp.where(jnp.asarray(live & (jj >= half)), sin[:, idx], 0.0)
    return jnp.stack([c, sa, sb])


def _rope_tables(pos):
    t = [_rope_pattern(pos, 128, 32, 32, 128),
         _rope_pattern(pos, 256, 32, 8, 256),
         _rope_pattern(pos, 128, 32, 8, 64),
         _rope_pattern(pos, 256, 64, 16, 256),
         _rope_pattern(pos, 128, 64, 16, 64),
         _rope_pattern(pos, 128, 32, 32, 32)]
    return jnp.concatenate(t, axis=-1)


def _layer_weights(l, cols, w_ada, b_ada, norm_g, w_in, mla_q_norm, mla_kv_norm, mla_w_uq, mla_w_uk, mla_w_uv,
                   diff_lambda, diff_subln, nsa_cmp_pos, nsa_cmp_w1, nsa_cmp_w2, w_out):
    wl = {}
    wl["w_ada"] = w_ada[l].astype(BF)
    wl["b_ada"] = b_ada[l][None, :]
    wl["g"] = norm_g[l][None, :]
    wl["wr"] = jnp.where(jnp.asarray(cols >= 0)[None, :], w_in[l][:, np.maximum(cols, 0)], 0.0).astype(BF)
    wl["qn"] = jnp.pad(mla_q_norm[l], (0, 256 - MLA_Q_RANK))[None, :]
    wl["kvn"] = mla_kv_norm[l][None, :]
    uq = jnp.pad(mla_w_uq[l], ((0, 256 - MLA_Q_RANK), (0, 0), (0, 0)))
    wl["wuqn"] = uq[:, :, :MLA_NOPE].reshape(256, HEADS * MLA_NOPE).astype(BF)
    wl["wuqp"] = uq[:, :, MLA_NOPE:].reshape(256, HEADS * MLA_ROPE).astype(BF)
    wuk = jnp.zeros((HEADS * MLA_NOPE, HEADS * MLA_KV_RANK), F32)
    wuv = jnp.zeros((HEADS, MLA_KV_RANK, GROUP_WIDTH), F32)
    for h in range(HEADS):
        wuk = wuk.at[h * MLA_NOPE:(h + 1) * MLA_NOPE, h * MLA_KV_RANK:(h + 1) * MLA_KV_RANK].set(mla_w_uk[l][:, h, :].T)
        wuv = wuv.at[h, :, h * HEAD_DIM:(h + 1) * HEAD_DIM].set(mla_w_uv[l][:, h, :])
    wl["wukbd"] = wuk.astype(BF)
    wl["wuv"] = wuv.astype(BF)
    wl["subln"] = jnp.pad(diff_subln[l], (HEAD_DIM, 0))[None, :]
    wl["lam"] = diff_lambda[l]
    w1 = nsa_cmp_w1[l].reshape(2, 2, NSA_CMP_STRIDE, NSA_DK, NSA_DK)
    w1s = jnp.zeros((NSA_CMP_STRIDE, 2, NSA_DK, 2, 2, NSA_DK), F32)
    for c in range(2):
        w1s = w1s.at[:, c, :, :, c, :].set(jnp.transpose(w1[c], (1, 2, 0, 3)))
    wl["w1s"] = w1s.reshape(NSA_CMP_STRIDE, 2 * NSA_DK, 4 * NSA_DK).astype(BF)
    pos = nsa_cmp_pos[l].reshape(2, 2, NSA_CMP_STRIDE, NSA_DK)
    pos8 = jnp.transpose(pos, (2, 1, 0, 3)).reshape(NSA_CMP_STRIDE, 2, 2 * NSA_DK)
    wl["pos8"] = jnp.pad(pos8, ((0, 0), (0, 6), (0, 0)))
    w2 = jnp.zeros((2 * NSA_DK, 2 * NSA_DK), F32)
    for c in range(2):
        w2 = w2.at[c * NSA_DK:(c + 1) * NSA_DK, c * NSA_DK:(c + 1) * NSA_DK].set(nsa_cmp_w2[l][c])
    wl["w2bd"] = w2.astype(BF)
    wl["wout"] = w_out[l].astype(BF)
    return wl


def _overlap(n_cmp, n_rows, n_sel, n_cols):
    i = np.arange(n_rows)[:, None]
    j = np.arange(n_cols)[None, :]
    cs, ss = i * NSA_CMP_STRIDE, j * NSA_SEL_BLOCK
    ov = (cs <= ss + NSA_SEL_BLOCK - 1) & (cs + NSA_CMP_BLOCK - 1 >= ss) & (i < n_cmp) & (j < n_sel)
    return ov.astype(np.float32)


def _round_up(x, m):
    return -(-x // m) * m


def kernel(x_prompt, x_sample, cache_mla, cache_diff, cache_nsa_cmp, cache_nsa_sel, cache_dsa_kv, cache_dsa_idx,
           state_nsa_win, page_table, c_prompt, c_sample, w_ada, b_ada, norm_g, w_in, mla_q_norm, mla_kv_norm,
           mla_w_uq, mla_w_uk, mla_w_uv, diff_lambda, diff_subln, nsa_cmp_pos, nsa_cmp_w1, nsa_cmp_w2, w_out,
           final_norm):
    b, s, d = x_prompt.shape
    bd, s_dec, _ = x_sample.shape
    depth = w_in.shape[0]
    n_pages, page = page_table.shape[1], cache_mla.shape[2]
    past = n_pages * page
    assert s_dec == 1 and page == LANES and state_nsa_win.shape[2] == NSA_WINDOW
    tq = 128
    tm_p = 256 if s % 256 == 0 else 128
    assert s % tq == 0 and s >= NSA_WINDOW + tq and bd % 8 == 0 and past % NSA_SEL_BLOCK == 0
    grp_cmp = grp_idx = min(16, n_pages)
    grp_att = min(8, n_pages)
    assert n_pages % grp_cmp == 0 and n_pages % grp_att == 0

    cols = _in_columns()
    tab_p = _rope_tables(jnp.arange(s))
    tab_s = jnp.broadcast_to(_rope_tables(past + jnp.arange(1)), (3, bd, T_END))
    pt_flat = page_table.reshape(-1)
    fn = final_norm[None, :]

    nch_p = s // NSA_CMP_STRIDE
    ncmp_p = nch_p - NSA_CMP_BLOCK // NSA_CMP_STRIDE + 1
    nsel_p = -(-s // NSA_SEL_BLOCK)
    nsp = _round_up(nsel_p, 8)
    ovt = jnp.asarray(_overlap(ncmp_p, nch_p, nsel_p, nsp).T, BF)
    emat = jnp.asarray((np.arange(s)[None, :] // NSA_SEL_BLOCK == np.arange(nsp)[:, None]).astype(np.float32), BF)
    ntop_p, ksel_p = min(NSA_TOPN, nsel_p), min(DSA_TOPK, s // 4)
    l_s = past + 1
    nch_s = l_s // NSA_CMP_STRIDE
    ncmp_s = nch_s - NSA_CMP_BLOCK // NSA_CMP_STRIDE + 1
    nsel_s = -(-l_s // NSA_SEL_BLOCK)
    nsl = _round_up(nsel_s, LANES)
    ov_s = jnp.asarray(_overlap(ncmp_s, nch_s, nsel_s, nsl), BF)
    ntop_s, ksel_s = min(NSA_TOPN, nsel_s), min(DSA_TOPK, l_s // 4)

    xp = x_prompt
    xs = x_sample.reshape(1, bd, d)
    rows_p, rows_s = [], []
    for l in range(depth):
        wl = _layer_weights(l, cols, w_ada, b_ada, norm_g, w_in, mla_q_norm, mla_kv_norm, mla_w_uq, mla_w_uk,
                            mla_w_uv, diff_lambda, diff_subln, nsa_cmp_pos, nsa_cmp_w1, nsa_cmp_w2, w_out)
        lam_init = 0.8 - 0.6 * math.exp(-0.3 * l)
        final = l == depth - 1
        shared = (wl["g"], wl["wr"], wl["qn"], wl["kvn"], wl["wuqn"], wl["wuqp"], wl["wukbd"])

        mod_p = _ada(c_prompt, wl["w_ada"], wl["b_ada"])[:, None, :]
        pr = _inproj(xp, mod_p, *shared, tab_p, tm_p)
        cmpkv_p = _cmp2(_cmp1_prompt(pr["cmp"], wl["w1s"]), wl["pos8"], wl["w1s"], wl["w2bd"])
        o_p = _prompt_attn(pr, cmpkv_p, wl["wuv"], wl["subln"], wl["lam"], ovt, emat, lam_init,
                           ncmp_p, nsel_p, ntop_p, ksel_p, tq)
        xp = _outproj(o_p, pr["z"], xp, mod_p, wl["wout"], fn, final, tm_p)
        rows_p.append(pr)

        mod_s = _ada(c_sample, wl["w_ada"], wl["b_ada"])[None]
        sr = _inproj(xs, mod_s, *shared, tab_s, bd)
        cmpkv_s = _cmp2(_cmp1_sample(cache_nsa_cmp, l, pt_flat, bd, n_pages, wl["w1s"], grp_cmp),
                        wl["pos8"], wl["w1s"], wl["w2bd"])

        def rows8(a):
            t = jnp.transpose(a[:, 0], (1, 0, 2))
            return jnp.pad(t, ((0, 0), (0, 8 - t.shape[1]), (0, 0)))
        iq8 = jnp.stack([sr["iq"][h, 0, :, h * DSA_IDX_DIM:(h + 1) * DSA_IDX_DIM] for h in range(HEADS)], axis=1)
        iq8 = jnp.pad(iq8, ((0, 0), (0, 8 - HEADS), (0, 0)))
        misc_s = sr["misc"][0]
        iw8 = jnp.pad(misc_s[:, M_IW:M_IW + HEADS], ((0, 0), (0, 8 - HEADS)))[:, :, None]
        tok = lambda name: sr[name][0][:, None, :]
        dmask, smask, ocmp = _sample_select(
            cache_dsa_idx, l, pt_flat, bd, n_pages, iq8, iw8, tok("idx"), rows8(sr["nq"]), cmpkv_s, ov_s,
            ncmp_s, nsel_s, ntop_s, ksel_s, past, grp_idx)
        per_b = [rows8(sr["qa"]), rows8(sr["dq"]), rows8(sr["nqr"]), rows8(sr["sq"]), misc_s[:, None, :],
                 ocmp, smask, dmask, tok("mla"), tok("diff"), tok("sel"), tok("win"), tok("kv"), state_nsa_win[l]]
        o_s = _sample_attn((cache_mla, cache_diff, cache_nsa_sel, cache_dsa_kv), l, pt_flat, bd, n_pages,
                           per_b, [wl["wuv"], wl["subln"], wl["lam"]], lam_init, past, grp_att)
        xs = _outproj(o_s.reshape(1, bd, 4 * GROUP_WIDTH), sr["z"], xs, mod_s, wl["wout"], fn, final, bd)
        rows_s.append(sr)

    def stack_p(name):
        return jnp.stack([r[name] for r in rows_p])

    def stack_s(name):
        return jnp.stack([r[name][0][:, None, :] for r in rows_s])
    win_p = jnp.stack([r["win"][:, s - min(NSA_WINDOW, s):] for r in rows_p])
    win_s = jnp.stack([jnp.concatenate([state_nsa_win[l][:, 1:], rows_s[l]["win"][0][:, None, :]], axis=1)
                       for l in range(depth)])
    return (xp, xs.reshape(bd, 1, d),
            stack_p("mla"), stack_s("mla"), stack_p("diff"), stack_s("diff"),
            stack_p("cmp"), stack_s("cmp"), stack_p("sel"), stack_s("sel"),
            stack_p("kv"), stack_s("kv"), stack_p("idx"), stack_s("idx"), win_p, win_s)
```

```python
import functools
import math

import numpy as np
import jax
import jax.numpy as jnp
from jax import lax
from jax.experimental import pallas as pl
from jax.experimental.pallas import tpu as pltpu

F32 = jnp.float32
BF = jnp.bfloat16

HEAD_DIM = 64
HEADS = 4
GROUP_WIDTH = HEADS * HEAD_DIM
ROPE_THETA = 500000.0
NEG = -1e30
BIG = 1e9
EPS = 1e-6
MLA_Q_RANK = 192
MLA_KV_RANK = 128
MLA_NOPE = 64
MLA_ROPE = 32
MLA_SCALE = (MLA_NOPE + MLA_ROPE) ** -0.5
DIFF_DK = 32
NSA_DK = 64
NSA_CMP_BLOCK = 32
NSA_CMP_STRIDE = 16
NSA_SEL_BLOCK = 64
NSA_TOPN = 16
NSA_LOCAL = 2
NSA_WINDOW = 512
DSA_DK = 64
DSA_IDX_DIM = 32
DSA_TOPK = 256
DSA_IDX_SCALE = (DSA_IDX_DIM ** -0.5) * (HEADS ** -0.5)
LOG2E = 1.4426950408889634
QS_MLA = MLA_SCALE * LOG2E
QS_DIFF = DIFF_DK ** -0.5 * LOG2E
QS_64 = HEAD_DIM ** -0.5 * LOG2E

LANES = 128
VMEM_LIMIT = 56 * 1024 * 1024
INT_MIN = -2 ** 31

P_CQ, P_CKV, P_MISC, P_Z, P_DQ, P_DIFFKV, P_NQ, P_CMP, P_SEL, P_WIN, P_SQ, P_DSAKV, P_IQ, P_IK4, P_END = (
    0, 256, 384, 512, 1536, 1792, 1920, 2176, 2304, 2432, 2560, 2816, 2944, 3072, 3200)
M_GATE, M_IW = 32, 44
T_QPE, T_H32, T_DIFFKV, T_H64, T_KV64, T_MISC, T_END = 0, 128, 384, 512, 768, 896, 1024


def _dot(a, b):
    return jnp.dot(a, b, preferred_element_type=F32)


def _dot_t(a, b):
    return lax.dot_general(a, b, (((1,), (1,)), ((), ())), preferred_element_type=F32)


def _silu(x):
    return x / (1.0 + jnp.exp(-x))


def _const_spec(a):
    nd = a.ndim
    return pl.BlockSpec(a.shape, lambda *_: (0,) * nd)


def _params(sem):
    return pltpu.CompilerParams(dimension_semantics=sem, vmem_limit_bytes=VMEM_LIMIT)


def _ada_body(c_ref, w_ref, b_ref, o_ref):
    o_ref[...] = _dot(c_ref[...].astype(BF), w_ref[...]) + b_ref[...]


def _ada(c, w_bf, b):
    return pl.pallas_call(
        _ada_body,
        out_shape=jax.ShapeDtypeStruct((c.shape[0], w_bf.shape[1]), F32),
        compiler_params=pltpu.CompilerParams(vmem_limit_bytes=VMEM_LIMIT),
        name="ada_mod",
    )(c, w_bf, b)


def _rope(x, tab_ref, off, half):
    w = x.shape[1]
    c = tab_ref[0, :, off:off + w]
    sa = tab_ref[1, :, off:off + w]
    sb = tab_ref[2, :, off:off + w]
    return x * c + pltpu.roll(x, w - half, 1) * sa + pltpu.roll(x, half, 1) * sb


def _inproj_body(x_ref, mod_ref, g_ref, w_ref, qn_ref, kvn_ref, wuqn_ref, wuqp_ref, wuk_ref, tab_ref,
                 qa_ref, dq_ref, nq_ref, nqr_ref, sq_ref, iq_ref, misc_ref, z_ref,
                 mla_ref, diff_ref, cmp_ref, sel_ref, win_ref, kv_ref, idx_ref,
                 mlab_ref, diffb_ref, selb_ref, winb_ref, kvb_ref, ik4b_ref):
    d = x_ref.shape[-1]
    x = x_ref[...]
    mod = mod_ref[...]
    shift, scale = mod[:, 0:d], mod[:, d:2 * d]
    xn = x * lax.rsqrt(jnp.mean(x * x, -1, keepdims=True) + EPS) * g_ref[...]
    hb = (xn * (1.0 + scale) + shift).astype(BF)
    lane = lax.broadcasted_iota(jnp.int32, (x.shape[0], LANES), 1)

    def proj(a, b):
        return _dot(hb, w_ref[:, a:b])

    cq = proj(P_CQ, P_CKV)
    cqn = (cq * lax.rsqrt(jnp.sum(cq * cq, -1, keepdims=True) * (1.0 / MLA_Q_RANK) + EPS) * qn_ref[...]).astype(BF)
    q_lat = _dot(_dot(cqn, wuqn_ref[...]).astype(BF), wuk_ref[...])
    q_pe = _rope(_dot(cqn, wuqp_ref[...]), tab_ref, T_QPE, MLA_ROPE // 2)
    for h in range(HEADS):
        qa_ref[h, :, 0:MLA_KV_RANK] = (q_lat[:, h * LANES:(h + 1) * LANES] * QS_MLA).astype(BF)
        qa_ref[h, :, MLA_KV_RANK:] = (q_pe[:, h * MLA_ROPE:(h + 1) * MLA_ROPE] * QS_MLA).astype(BF)
    ckv = proj(P_CKV, P_MISC)
    c_new = ckv * lax.rsqrt(jnp.mean(ckv * ckv, -1, keepdims=True) + EPS) * kvn_ref[...]
    misc = proj(P_MISC, P_Z)
    misc_r = _rope(misc, tab_ref, T_MISC, MLA_ROPE // 2)
    mla_ref[:, 0:MLA_KV_RANK] = c_new
    mla_ref[:, MLA_KV_RANK:] = misc_r[:, 0:MLA_ROPE]
    mlab_ref[:, 0:MLA_KV_RANK] = c_new.astype(BF)
    mlab_ref[:, MLA_KV_RANK:] = misc_r[:, 0:MLA_ROPE].astype(BF)
    is_gate = (lane >= M_GATE) & (lane < M_GATE + 3 * HEADS)
    misc_ref[...] = jnp.where(is_gate, 1.0 / (1.0 + jnp.exp(-misc)), misc_r)
    z_ref[...] = proj(P_Z, P_DQ)

    dq = _rope(proj(P_DQ, P_DIFFKV), tab_ref, T_H32, DIFF_DK // 8) * QS_DIFF
    for h in range(HEADS):
        slab = dq[:, (h // 2) * LANES:(h // 2 + 1) * LANES]
        if h % 2:
            slab = pltpu.roll(slab, HEAD_DIM, 1)
        dq_ref[h] = jnp.where(lane < DIFF_DK, slab, 0.0).astype(BF)
        dq_ref[HEADS + h] = jnp.where((lane >= DIFF_DK) & (lane < 2 * DIFF_DK), slab, 0.0).astype(BF)
    diff_row = _rope(proj(P_DIFFKV, P_NQ), tab_ref, T_DIFFKV, DIFF_DK // 8)
    diff_ref[...] = diff_row
    diffb_ref[...] = diff_row.astype(BF)

    def per_head(v, out_ref):
        for h in range(HEADS):
            slab = v[:, (h // 2) * LANES:(h // 2 + 1) * LANES]
            if h % 2:
                slab = pltpu.roll(slab, HEAD_DIM, 1)
            out_ref[h] = jnp.where(lane < HEAD_DIM, slab, 0.0).astype(BF)

    nq = proj(P_NQ, P_CMP) * QS_64
    per_head(nq, nq_ref)
    per_head(_rope(nq, tab_ref, T_H64, NSA_DK // 8), nqr_ref)
    cmp_ref[...] = proj(P_CMP, P_SEL)
    sel_row = _rope(proj(P_SEL, P_WIN), tab_ref, T_KV64, NSA_DK // 8)
    sel_ref[...] = sel_row
    selb_ref[...] = sel_row.astype(BF)
    win_row = _rope(proj(P_WIN, P_SQ), tab_ref, T_KV64, NSA_DK // 8)
    win_ref[...] = win_row
    winb_ref[...] = win_row.astype(BF)

    per_head(_rope(proj(P_SQ, P_DSAKV) * QS_64, tab_ref, T_H64, DSA_DK // 8), sq_ref)
    kv_row = _rope(proj(P_DSAKV, P_IQ), tab_ref, T_KV64, DSA_DK // 8)
    kv_ref[...] = kv_row
    kvb_ref[...] = kv_row.astype(BF)
    iq = _rope(proj(P_IQ, P_IK4), tab_ref, T_H32, DSA_IDX_DIM // 8)
    for h in range(HEADS):
        live = (lane >= h * DSA_IDX_DIM) & (lane < (h + 1) * DSA_IDX_DIM)
        iq_ref[h] = jnp.where(live, iq, 0.0).astype(BF)
    ik4 = _rope(proj(P_IK4, P_END), tab_ref, T_H32, DSA_IDX_DIM // 8)
    idx_ref[...] = ik4[:, 0:DSA_IDX_DIM]
    ik4b_ref[...] = ik4.astype(BF)


def _inproj(x3, mod3, g, wr, qn, kvn, wuqn, wuqp, wukbd, tab, tm):
    bk, sk, d = x3.shape
    per_token = mod3.shape[1] != 1
    grid = (sk // tm, bk)
    tok = lambda s, b: (b, s, 0)
    hed = lambda s, b: (0, b, s, 0)
    in_specs = [
        pl.BlockSpec((None, tm, d), tok),
        pl.BlockSpec((None, tm, 3 * d), tok) if per_token else pl.BlockSpec((None, 1, 3 * d), lambda s, b: (b, 0, 0)),
        _const_spec(g), _const_spec(wr), _const_spec(qn), _const_spec(kvn),
        _const_spec(wuqn), _const_spec(wuqp), _const_spec(wukbd),
        pl.BlockSpec((3, tm, T_END), lambda s, b: (0, s, 0)),
    ]

    def hs(n, w):
        return jax.ShapeDtypeStruct((n, bk, sk, w), BF), pl.BlockSpec((n, None, tm, w), hed)

    def ts(w, dt):
        return jax.ShapeDtypeStruct((bk, sk, w), dt), pl.BlockSpec((None, tm, w), tok)

    outs = [hs(HEADS, MLA_KV_RANK + MLA_ROPE), hs(2 * HEADS, LANES), hs(HEADS, LANES), hs(HEADS, LANES),
            hs(HEADS, LANES), hs(HEADS, LANES), ts(LANES, F32), ts(4 * GROUP_WIDTH, F32),
            ts(MLA_KV_RANK + MLA_ROPE, F32), ts(LANES, F32), ts(LANES, F32), ts(LANES, F32), ts(LANES, F32),
            ts(LANES, F32), ts(DSA_IDX_DIM, F32),
            ts(MLA_KV_RANK + MLA_ROPE, BF), ts(LANES, BF), ts(LANES, BF), ts(LANES, BF), ts(LANES, BF),
            ts(LANES, BF)]
    names = ("qa", "dq", "nq", "nqr", "sq", "iq", "misc", "z", "mla", "diff", "cmp", "sel", "win", "kv", "idx",
             "mla_b", "diff_b", "sel_b", "win_b", "kv_b", "ik4_b")
    res = pl.pallas_call(
        _inproj_body,
        out_shape=[o[0] for o in outs],
        grid=grid,
        in_specs=in_specs,
        out_specs=[o[1] for o in outs],
        compiler_params=_params(("arbitrary", "arbitrary")),
        name="in_proj",
    )(x3, mod3, g, wr, qn, kvn, wuqn, wuqp, wukbd, tab)
    return dict(zip(names, res))


def _chunk_proj(row_refs, w1s_ref):
    acc = None
    for s in range(NSA_CMP_STRIDE):
        parts = [r[pl.ds(s, r.shape[0] // NSA_CMP_STRIDE, stride=NSA_CMP_STRIDE), :] for r in row_refs]
        xs = parts[0] if len(parts) == 1 else jnp.concatenate(parts, axis=0)
        t = _dot(xs.astype(BF), w1s_ref[s])
        acc = t if acc is None else acc + t
    return acc


def _cmp1_prompt_body(rows_ref, w1s_ref, u_ref):
    u_ref[...] = _chunk_proj([rows_ref], w1s_ref)


def _cmp1_prompt(cmp_rows, w1s):
    b, s, _ = cmp_rows.shape
    return pl.pallas_call(
        _cmp1_prompt_body,
        out_shape=jax.ShapeDtypeStruct((b, s // NSA_CMP_STRIDE, 2 * LANES), F32),
        grid=(b,),
        in_specs=[pl.BlockSpec((None, s, LANES), lambda i: (i, 0, 0)), _const_spec(w1s)],
        out_specs=pl.BlockSpec((None, s // NSA_CMP_STRIDE, 2 * LANES), lambda i: (i, 0, 0)),
        compiler_params=_params(("arbitrary",)),
        name="nsa_cmp1_prompt",
    )(cmp_rows, w1s)


def _page_specs(cache, layer, n_pages, group, n_extra):
    def make(k):
        def imap(b, g, pt):
            return (layer, pt[b * n_pages + g * group + k], 0, 0)
        return pl.BlockSpec((None, None) + cache.shape[2:], imap)
    return [make(k) for k in range(group)]


def _cmp1_sample_body(group, pt_ref, *refs):
    pages, w1s_ref, u_ref = refs[:group], refs[group], refs[group + 1]
    u_ref[...] = _chunk_proj(list(pages), w1s_ref)


def _cmp1_sample(cache, layer, pt_flat, bd, n_pages, w1s, group):
    page = cache.shape[2]
    cpp = page // NSA_CMP_STRIDE
    gs = pltpu.PrefetchScalarGridSpec(
        num_scalar_prefetch=1,
        grid=(bd, n_pages // group),
        in_specs=_page_specs(cache, layer, n_pages, group, 0) + [pl.BlockSpec(w1s.shape, lambda b, g, pt: (0, 0, 0))],
        out_specs=pl.BlockSpec((None, group * cpp, 2 * LANES), lambda b, g, pt: (b, g, 0)),
    )
    return pl.pallas_call(
        functools.partial(_cmp1_sample_body, group),
        out_shape=jax.ShapeDtypeStruct((bd, n_pages * cpp, 2 * LANES), F32),
        grid_spec=gs,
        compiler_params=_params(("arbitrary", "arbitrary")),
        name="nsa_cmp1_sample",
    )(pt_flat, *([cache] * group), w1s)


def _cmp2_body(u_ref, pos_ref, w1s_ref, w2_ref, o_ref):
    n = u_ref.shape[0]
    pacc = None
    for s in range(NSA_CMP_STRIDE):
        t = _dot(pos_ref[s].astype(BF), w1s_ref[s])
        pacc = t if pacc is None else pacc + t
    posterm = pacc[0:1, 0:LANES] + pacc[1:2, LANES:2 * LANES]
    u = u_ref[...]
    pre = u[:, 0:LANES] + pltpu.roll(u[:, LANES:2 * LANES], n - 1, 0) + posterm
    o_ref[...] = _dot(_silu(pre).astype(BF), w2_ref[...])


def _cmp2(u, pos8, w1s, w2bd):
    b, n, _ = u.shape
    return pl.pallas_call(
        _cmp2_body,
        out_shape=jax.ShapeDtypeStruct((b, n, LANES), F32),
        grid=(b,),
        in_specs=[pl.BlockSpec((None, n, 2 * LANES), lambda i: (i, 0, 0)),
                  _const_spec(pos8), _const_spec(w1s), _const_spec(w2bd)],
        out_specs=pl.BlockSpec((None, n, LANES), lambda i: (i, 0, 0)),
        compiler_params=_params(("arbitrary",)),
        name="nsa_cmp2",
    )(u, pos8, w1s, w2bd)


def _softmax_parts(s, mask):
    s = jnp.where(mask, s, NEG)
    m = jnp.max(s, -1, keepdims=True)
    p = jnp.where(mask, jnp.exp2(s - m), 0.0)
    return p, 1.0 / jnp.maximum(jnp.sum(p, -1, keepdims=True), 1e-30)


def _softmax_bias(s, bias):
    s = s + bias
    p = jnp.exp2(s - jnp.max(s, -1, keepdims=True))
    return p, 1.0 / jnp.maximum(jnp.sum(p, -1, keepdims=True), 1e-30)


def _lambda(lam_ref, lam_init):
    lp = lam_ref[...]
    a = jnp.sum(lp[0:1] * lp[1:2], -1, keepdims=True)
    b = jnp.sum(lp[2:3] * lp[3:4], -1, keepdims=True)
    return jnp.exp(a) - jnp.exp(b) + lam_init


def _sort_key(x):
    bits = lax.bitcast_convert_type(x, jnp.int32)
    return jnp.where(bits < 0, bits ^ jnp.int32(0x7FFFFFFF), bits)


def _kth_largest_key(key, k, count):
    rows = count(key >= 0)
    t = jnp.where(rows >= k, jnp.int32(0), jnp.int32(INT_MIN))

    def step(i, t):
        cand = t + jnp.left_shift(jnp.int32(1), 30 - i)
        return jnp.where(count(key >= cand) >= k, cand, t)
    return lax.fori_loop(0, 31, step, t)


def _wrap32(v):
    return jnp.int32((v + 2 ** 31) % 2 ** 32 - 2 ** 31)


def _kth_largest_key_nibbles(key, k, count):
    t = jnp.full((1, 1), INT_MIN, jnp.int32)
    for shift in range(28, -1, -4):
        n_ok = jnp.zeros((1, 1), jnp.int32)
        for c in range(1, 16):
            ok = count(key >= t + _wrap32(c << shift)) >= k
            n_ok = n_ok + jnp.where(ok, 1, 0)
        t = t + n_ok * _wrap32(1 << shift)
    return t


def _tri_consts():
    r = lax.broadcasted_iota(jnp.int32, (LANES, LANES), 0)
    c = lax.broadcasted_iota(jnp.int32, (LANES, LANES), 1)
    return jnp.where(r < c, 1.0, 0.0).astype(BF), jnp.ones((LANES, LANES), BF)


def _place_heads(parts, lane):
    slabs = []
    for m in range(HEADS // 2):
        slabs.append(jnp.where(lane < HEAD_DIM, pltpu.roll(parts[2 * m], HEAD_DIM, 1), parts[2 * m + 1]))
    return jnp.concatenate(slabs, axis=1)


def _prompt_attn_body(q_tile0, n_cmp, n_sel, n_top, k_sel, lam_init,
                      qa_ref, dq_ref, nq_ref, nqr_ref, sq_ref, iq_ref, misc_ref,
                      mla_ref, diff_ref, cmpkv_ref, sel_ref, win_ref, kv_ref, ik4_ref,
                      wuv_ref, subln_ref, lam_ref, ovt_ref, e_ref, o_ref):
    tq = misc_ref.shape[0]
    s_len = mla_ref.shape[0]
    nsp = ovt_ref.shape[0]
    q0 = (pl.program_id(1) + q_tile0) * tq
    qpos = q0 + lax.broadcasted_iota(jnp.int32, (tq, 1), 0)
    kpos = lax.broadcasted_iota(jnp.int32, (1, s_len), 1)
    cbias = jnp.where(kpos <= qpos, 0.0, NEG)
    lane = lax.broadcasted_iota(jnp.int32, (tq, LANES), 1)
    misc = misc_ref[...]
    lam = _lambda(lam_ref, lam_init)

    kc = mla_ref[:, 0:MLA_KV_RANK]
    o_a = None
    for h in range(HEADS):
        p, r = _softmax_bias(_dot_t(qa_ref[h], mla_ref[...]), cbias)
        o_lat = _dot(p.astype(BF), kc) * r
        t = _dot(o_lat.astype(BF), wuv_ref[h])
        o_a = t if o_a is None else o_a + t
    o_ref[:, 0:GROUP_WIDTH] = o_a

    kd = diff_ref[...]
    parts = []
    for h in range(HEADS):
        p1, r1 = _softmax_bias(_dot_t(dq_ref[h], kd), cbias)
        p2, r2 = _softmax_bias(_dot_t(dq_ref[HEADS + h], kd), cbias)
        o = _dot(p1.astype(BF), kd) * r1 - lam * (_dot(p2.astype(BF), kd) * r2)
        ms = jnp.sum(jnp.where(lane >= HEAD_DIM, o * o, 0.0), -1, keepdims=True) * (1.0 / HEAD_DIM)
        parts.append(o * lax.rsqrt(ms + EPS) * subln_ref[...] * (1.0 - lam_init))
    o_ref[:, GROUP_WIDTH:2 * GROUP_WIDTH] = _place_heads(parts, lane)

    ckv = cmpkv_ref[...].astype(BF)
    ncol = lax.broadcasted_iota(jnp.int32, (1, ckv.shape[0]), 1)
    cmask = (ncol * NSA_CMP_STRIDE + (NSA_CMP_BLOCK - 1) <= qpos) & (ncol < n_cmp)
    o_cmp, pcsum = [], None
    for h in range(HEADS):
        p, r = _softmax_parts(_dot_t(nq_ref[h], ckv), cmask)
        pc = p * r
        o_cmp.append(_dot(pc.astype(BF), ckv))
        pcsum = pc if pcsum is None else pcsum + pc
    hi = pcsum.astype(BF)
    lo = (pcsum - hi.astype(F32)).astype(BF)
    imp_t = _dot_t(ovt_ref[...], hi) + _dot_t(ovt_ref[...], lo)
    jb = lax.broadcasted_iota(jnp.int32, (nsp, tq), 0)
    qblk = (q0 + lax.broadcasted_iota(jnp.int32, (1, tq), 1)) // NSA_SEL_BLOCK
    forced = (jb == 0) | (jb > qblk - NSA_LOCAL)
    score = jnp.where(jb > qblk, NEG, jnp.where(forced, BIG, imp_t))
    score = jnp.where(jb < n_sel, score, -jnp.inf)
    sel_t = jnp.zeros((nsp, tq), F32)
    for _ in range(n_top):
        m = jnp.max(score, 0, keepdims=True)
        first = jnp.min(jnp.where(score == m, jb, nsp), 0, keepdims=True)
        hit = jb == first
        sel_t = jnp.where(hit, 1.0, sel_t)
        score = jnp.where(hit, -jnp.inf, score)
    sel_keys = lax.dot_general(sel_t.astype(BF), e_ref[...], (((0,), (0,)), ((), ())),
                               preferred_element_type=F32)
    sbias = jnp.where(sel_keys > 0.5, cbias, NEG)
    ks = sel_ref[...]
    o_sel = []
    for h in range(HEADS):
        p, r = _softmax_bias(_dot_t(nqr_ref[h], ks), sbias)
        o_sel.append(_dot(p.astype(BF), ks) * r)
    span = min(NSA_WINDOW + tq, s_len)
    start = pl.multiple_of(jnp.maximum(q0 - NSA_WINDOW, 0), tq)
    kw = win_ref[pl.ds(start, span), :]
    wpos = start + lax.broadcasted_iota(jnp.int32, (1, span), 1)
    wbias = jnp.where((wpos <= qpos) & (wpos > qpos - NSA_WINDOW), 0.0, NEG)
    parts = []
    for h in range(HEADS):
        p, r = _softmax_bias(_dot_t(nqr_ref[h], kw), wbias)
        o_win = _dot(p.astype(BF), kw) * r
        g = [misc[:, M_GATE + HEADS * i + h:M_GATE + HEADS * i + h + 1] for i in range(3)]
        parts.append(g[0] * o_cmp[h] + g[1] * o_sel[h] + g[2] * o_win)
    o_ref[:, 2 * GROUP_WIDTH:3 * GROUP_WIDTH] = _place_heads(parts, lane)

    ki = ik4_ref[...]
    isc = None
    for h in range(HEADS):
        t = jnp.maximum(_dot_t(iq_ref[h], ki), 0.0) * misc[:, M_IW + h:M_IW + h + 1]
        isc = t if isc is None else isc + t
    isc = jnp.where(kpos <= qpos, isc * DSA_IDX_SCALE + 0.0, NEG)
    key = _sort_key(isc)
    count = lambda pred: jnp.sum(jnp.where(pred, 1.0, 0.0), -1, keepdims=True)
    kk = float(min(k_sel, s_len))
    kth = _kth_largest_key(key, kk, count)
    need = kk - count(key > kth)
    tri, ones = _tri_consts()
    offs = jnp.zeros((tq, LANES), F32)
    chunks = []
    for c in range(s_len // LANES):
        kc_ = key[:, c * LANES:(c + 1) * LANES]
        eq = jnp.where(kc_ == kth, 1.0, 0.0).astype(BF)
        before = _dot(eq, tri) + offs
        offs = offs + _dot(eq, ones)
        keep = (kc_ > kth) | ((kc_ == kth) & (before < need))
        chunks.append(jnp.where(keep, cbias[:, c * LANES:(c + 1) * LANES], NEG))
    dbias = jnp.concatenate(chunks, axis=1)
    kv = kv_ref[...]
    parts = []
    for h in range(HEADS):
        p, r = _softmax_bias(_dot_t(sq_ref[h], kv), dbias)
        parts.append(_dot(p.astype(BF), kv) * r)
    o_ref[:, 3 * GROUP_WIDTH:] = _place_heads(parts, lane)


def _prompt_attn(pr, cmpkv, wuv, subln, lam_p, ovt, emat, lam_init, n_cmp, n_sel, n_top, k_sel, tq, n_bucket):
    _, b, s, _ = pr["qa"].shape
    qb = s // n_bucket
    seq = lambda i, j: (i, 0, 0)
    qnames = ("qa", "dq", "nq", "nqr", "sq", "iq")
    knames = ("mla_b", "diff_b", None, "sel_b", "win_b", "kv_b", "ik4_b")
    kargs = [cmpkv if n is None else pr[n] for n in knames]
    outs = []
    for bk in range(n_bucket):
        t0, kv_len = bk * (qb // tq), (bk + 1) * qb

        def hs(a, t0=t0):
            return pl.BlockSpec((a.shape[0], None, tq, a.shape[3]), lambda i, j: (0, i, j + t0, 0))

        def ks(a, name, kv_len=kv_len):
            return pl.BlockSpec((None, a.shape[1] if name is None else kv_len, a.shape[2]), seq)
        consts = [wuv, subln, lam_p, ovt, emat[:, :kv_len]]
        outs.append(pl.pallas_call(
            functools.partial(_prompt_attn_body, t0, n_cmp, n_sel, n_top, k_sel, lam_init),
            out_shape=jax.ShapeDtypeStruct((b, qb, 4 * GROUP_WIDTH), F32),
            grid=(b, qb // tq),
            in_specs=[hs(pr[n]) for n in qnames]
            + [pl.BlockSpec((None, tq, LANES), lambda i, j, t0=t0: (i, j + t0, 0))]
            + [ks(a, n) for a, n in zip(kargs, knames)] + [_const_spec(c) for c in consts],
            out_specs=pl.BlockSpec((None, tq, 4 * GROUP_WIDTH), lambda i, j: (i, j, 0)),
            compiler_params=_params(("arbitrary", "arbitrary")),
            name="prompt_mixers",
        )(*[pr[n] for n in qnames], pr["misc"], *kargs, *consts))
    return outs[0] if n_bucket == 1 else jnp.concatenate(outs, axis=1)


def _sample_select_body(group, n_pages, n_cmp, n_sel, n_top, k_sel, qpos, pt_ref, *refs):
    pages = refs[:group]
    (iq_ref, iw_ref, ikn_ref, nq_ref, cmpkv_ref, ov_ref,
     dmask_ref, smask_ref, ocmp_ref, isc_ref) = refs[group:]
    g = pl.program_id(1)
    page = pages[0].shape[1]
    rp = isc_ref.shape[0]
    q = iq_ref[...]
    w = iw_ref[...]
    sub = lax.broadcasted_iota(jnp.int32, (group, page), 0)
    rows = jnp.zeros((group, page), F32)
    kb = jnp.concatenate([pg[...].astype(BF) for pg in pages], axis=1)
    sc = jnp.sum(jnp.maximum(_dot(q, kb), 0.0) * w, 0, keepdims=True) * DSA_IDX_SCALE + 0.0
    for k in range(group):
        rows = jnp.where(sub == k, sc[:, k * page:(k + 1) * page], rows)
    isc_ref[pl.ds(pl.multiple_of(g * group, group), group), :] = rows

    @pl.when(g == pl.num_programs(1) - 1)
    def _():
        subt = lax.broadcasted_iota(jnp.int32, (rp - n_pages, page), 0)
        lanet = lax.broadcasted_iota(jnp.int32, (rp - n_pages, page), 1)
        kn = ikn_ref[...].astype(BF).astype(F32)
        sn = jnp.maximum(jnp.sum(q.astype(F32) * kn, -1, keepdims=True), 0.0) * w
        rn = jnp.sum(sn, 0, keepdims=True) * DSA_IDX_SCALE + 0.0
        isc_ref[n_pages:rp, :] = jnp.where((subt == 0) & (lanet == 0), rn, -jnp.inf)
        key = _sort_key(isc_ref[...])
        live = key[0:n_pages + 8]
        count = lambda pred: jnp.sum(jnp.sum(jnp.where(pred, 1.0, 0.0), 0, keepdims=True), 1, keepdims=True)
        kth = _kth_largest_key_nibbles(live, float(k_sel), count)
        need = float(k_sel) - count(live > kth)
        tri, ones = _tri_consts()
        eq = jnp.where(key == kth, 1.0, 0.0).astype(BF)
        rr = lax.broadcasted_iota(jnp.int32, (rp, rp), 0)
        cc = lax.broadcasted_iota(jnp.int32, (rp, rp), 1)
        rows_before = _dot(jnp.where(cc < rr, 1.0, 0.0).astype(BF), _dot(eq, ones).astype(BF))
        before = _dot(eq, tri) + rows_before
        dmask_ref[...] = jnp.where((key > kth) | ((key == kth) & (before < need)), 1.0, 0.0)
        ckv = cmpkv_ref[...].astype(BF)
        ncol = lax.broadcasted_iota(jnp.int32, (1, ckv.shape[0]), 1)
        cmask = (ncol * NSA_CMP_STRIDE + (NSA_CMP_BLOCK - 1) <= qpos) & (ncol < n_cmp)
        p, r = _softmax_parts(_dot_t(nq_ref[...], ckv), cmask)
        pc = p * r
        ocmp_ref[...] = _dot(pc.astype(BF), ckv)
        head_rows = lax.broadcasted_iota(jnp.int32, pc.shape, 0) < HEADS
        pcs = jnp.where(head_rows, pc, 0.0)
        hi = pcs.astype(BF)
        lo = (pcs - hi.astype(F32)).astype(BF)
        imp = jnp.sum(_dot(hi, ov_ref[...]) + _dot(lo, ov_ref[...]), 0, keepdims=True)
        nsl = imp.shape[1]
        jb = lax.broadcasted_iota(jnp.int32, (1, nsl), 1)
        qblk = qpos // NSA_SEL_BLOCK
        forced = (jb == 0) | (jb > qblk - NSA_LOCAL)
        score = jnp.where(jb > qblk, NEG, jnp.where(forced, BIG, imp))
        score = jnp.where(jb < n_sel, score, -jnp.inf)
        sel = jnp.zeros((1, nsl), F32)
        for _ in range(n_top):
            m = jnp.max(score, 1, keepdims=True)
            first = jnp.min(jnp.where(score == m, jb, nsl), 1, keepdims=True)
            hit = jb == first
            sel = jnp.where(hit, 1.0, sel)
            score = jnp.where(hit, -jnp.inf, score)
        bpp = page // NSA_SEL_BLOCK
        pr_ = lax.broadcasted_iota(jnp.int32, (rp, nsl), 0)
        pj = lax.broadcasted_iota(jnp.int32, (rp, nsl), 1)
        a = jnp.where(pj // bpp == pr_, sel, 0.0).astype(BF)
        fj = lax.broadcasted_iota(jnp.int32, (nsl, page), 0)
        fc = lax.broadcasted_iota(jnp.int32, (nsl, page), 1)
        f = jnp.where(fj % bpp == fc // NSA_SEL_BLOCK, 1.0, 0.0).astype(BF)
        smask_ref[...] = _dot(a, f)


def _sample_select(cache_idx, layer, pt_flat, bd, n_pages, iq8, iw8, ik_new, nq8, cmpkv, ov,
                   n_cmp, n_sel, n_top, k_sel, qpos, group):
    page = cache_idx.shape[3]
    rp = _round_up(n_pages + 1, LANES)
    per_b = lambda b, g, pt: (b, 0, 0)

    def bs(a):
        return pl.BlockSpec((None,) + a.shape[1:], per_b)
    gs = pltpu.PrefetchScalarGridSpec(
        num_scalar_prefetch=1,
        grid=(bd, n_pages // group),
        in_specs=_page_specs(cache_idx, layer, n_pages, group, 0)
        + [bs(iq8), bs(iw8), bs(ik_new), bs(nq8), bs(cmpkv), pl.BlockSpec(ov.shape, lambda b, g, pt: (0, 0))],
        out_specs=[pl.BlockSpec((None, rp, page), per_b), pl.BlockSpec((None, rp, page), per_b),
                   pl.BlockSpec((None, 8, LANES), per_b)],
        scratch_shapes=[pltpu.VMEM((rp, page), F32)],
    )
    return pl.pallas_call(
        functools.partial(_sample_select_body, group, n_pages, n_cmp, n_sel, n_top, k_sel, qpos),
        out_shape=[jax.ShapeDtypeStruct((bd, rp, page), F32), jax.ShapeDtypeStruct((bd, rp, page), F32),
                   jax.ShapeDtypeStruct((bd, 8, LANES), F32)],
        grid_spec=gs,
        compiler_params=_params(("arbitrary", "arbitrary")),
        name="sample_select",
    )(pt_flat, *([cache_idx] * group), iq8, iw8, ik_new, nq8, cmpkv, ov)


def _online_update(q, pages, transposed, mask, m_ref, l_ref, acc_ref):
    kb = jnp.concatenate(pages, axis=1 if transposed else 0)
    s = _dot(q, kb) if transposed else _dot_t(q, kb)
    if mask is not None:
        mk = mask > 0.5
        s = jnp.where(mk, s, NEG)
    m_old = m_ref[...]
    m_new = jnp.maximum(m_old, jnp.max(s, -1, keepdims=True))
    alpha = jnp.exp2(m_old - m_new)
    p = jnp.exp2(s - m_new)
    if mask is not None:
        p = jnp.where(mk, p, 0.0)
    l_ref[...] = alpha * l_ref[...] + jnp.sum(p, -1, keepdims=True)
    pb = p.astype(BF)
    pv = _dot_t(pb, kb) if transposed else _dot(pb, kb)
    acc_ref[...] = alpha * acc_ref[...] + pv
    m_ref[...] = m_new


def _merge_new_key(q, row, valid, m, l, acc):
    rb = row.astype(BF).astype(F32)
    s = jnp.sum(q.astype(F32) * rb, -1, keepdims=True)
    s = jnp.where(valid, s, NEG)
    m_new = jnp.maximum(m, s)
    alpha = jnp.exp2(m - m_new)
    p = jnp.where(valid, jnp.exp2(s - m_new), 0.0)
    l_new = alpha * l + p
    acc_new = alpha * acc + p.astype(BF).astype(F32) * rb
    return acc_new / jnp.maximum(l_new, 1e-30)


def _sample_attn_body(group, n_pages, lam_init, qpos, pt_ref, *refs):
    mla_pg, diff_pg = refs[0:group], refs[group:2 * group]
    sel_pg, kv_pg = refs[2 * group:3 * group], refs[3 * group:4 * group]
    (qa_ref, dq_ref, nqr_ref, sq_ref, misc_ref, ocmp_ref, smask_ref, dmask_ref,
     mlan_ref, diffn_ref, seln_ref, winn_ref, kvn_ref, wbuf_ref,
     wuv_ref, subln_ref, lam_ref, o_ref,
     ma, la, acca, mb, lb, accb, mc, lc, accc, md, ld, accd) = refs[4 * group:]
    g = pl.program_id(1)
    states = ((ma, la, acca), (mb, lb, accb), (mc, lc, accc), (md, ld, accd))

    @pl.when(g == 0)
    def _():
        for m_ref, l_ref, acc_ref in states:
            m_ref[...] = jnp.full(m_ref.shape, NEG, F32)
            l_ref[...] = jnp.zeros(l_ref.shape, F32)
            acc_ref[...] = jnp.zeros(acc_ref.shape, F32)

    def bf_pages(pgs):
        return [p[...].astype(BF) for p in pgs]

    def group_mask(mref):
        return mref[:, pl.ds(pl.multiple_of(g * (group * LANES), group * LANES), group * LANES)]

    _online_update(qa_ref[...], bf_pages(mla_pg), True, None, ma, la, acca)
    _online_update(dq_ref[...], bf_pages(diff_pg), False, None, mb, lb, accb)
    _online_update(nqr_ref[...], bf_pages(sel_pg), False, group_mask(smask_ref), mc, lc, accc)
    _online_update(sq_ref[...], bf_pages(kv_pg), False, group_mask(dmask_ref), md, ld, accd)

    @pl.when(g == pl.num_programs(1) - 1)
    def _():
        lane = lax.broadcasted_iota(jnp.int32, (8, LANES), 1)
        rowi = lax.broadcasted_iota(jnp.int32, (8, LANES), 0)
        misc = misc_ref[...]
        lam = _lambda(lam_ref, lam_init)
        true8 = jnp.full((8, 1), True)
        oa = _merge_new_key(qa_ref[...], mlan_ref[...], true8, ma[...], la[...], acca[...])
        o_lat = oa[:, 0:MLA_KV_RANK]
        o_a = None
        for h in range(HEADS):
            t = _dot(jnp.where(rowi == h, o_lat, 0.0).astype(BF), wuv_ref[h])
            o_a = t if o_a is None else o_a + t
        o_ref[:, 0:GROUP_WIDTH] = jnp.sum(o_a, 0, keepdims=True)
        ob = _merge_new_key(dq_ref[...], diffn_ref[...], true8, mb[...], lb[...], accb[...])
        o = ob - lam * pltpu.roll(ob, HEADS, 0)
        ms = jnp.sum(jnp.where(lane >= HEAD_DIM, o * o, 0.0), -1, keepdims=True) * (1.0 / HEAD_DIM)
        o = o * lax.rsqrt(ms + EPS) * subln_ref[...] * (1.0 - lam_init)
        o_ref[:, GROUP_WIDTH:2 * GROUP_WIDTH] = _place_heads([o[h:h + 1] for h in range(HEADS)], lane[0:1])
        new_at = n_pages * LANES
        snew = smask_ref[:, new_at:new_at + 1] > 0.5
        osel = _merge_new_key(nqr_ref[...], seln_ref[...], snew, mc[...], lc[...], accc[...])
        wb = wbuf_ref[...].astype(BF)
        nwin = wb.shape[0]
        wpos = qpos - nwin + lax.broadcasted_iota(jnp.int32, (1, nwin), 1)
        wmask = (wpos <= qpos) & (wpos > qpos - NSA_WINDOW) & (wpos >= 0)
        sw = jnp.where(wmask, _dot_t(nqr_ref[...], wb), NEG)
        mw = jnp.max(sw, -1, keepdims=True)
        pw = jnp.where(wmask, jnp.exp2(sw - mw), 0.0)
        owin = _merge_new_key(nqr_ref[...], winn_ref[...], true8, mw,
                              jnp.sum(pw, -1, keepdims=True), _dot(pw.astype(BF), wb))
        ocmp = ocmp_ref[...]
        parts = []
        for h in range(HEADS):
            gt = [misc[:, M_GATE + HEADS * i + h:M_GATE + HEADS * i + h + 1] for i in range(3)]
            parts.append(gt[0] * ocmp[h:h + 1] + gt[1] * osel[h:h + 1] + gt[2] * owin[h:h + 1])
        o_ref[:, 2 * GROUP_WIDTH:3 * GROUP_WIDTH] = _place_heads(parts, lane[0:1])
        dnew = dmask_ref[:, new_at:new_at + 1] > 0.5
        od = _merge_new_key(sq_ref[...], kvn_ref[...], dnew, md[...], ld[...], accd[...])
        o_ref[:, 3 * GROUP_WIDTH:] = _place_heads([od[h:h + 1] for h in range(HEADS)], lane[0:1])


def _sample_attn(caches, layer, pt_flat, bd, n_pages, per_b_args, consts, lam_init, qpos, group):
    per_b = lambda b, g, pt: (b, 0, 0)

    def bs(a):
        return pl.BlockSpec((None,) + a.shape[1:], per_b)

    def cs(a):
        nd = a.ndim
        return pl.BlockSpec(a.shape, lambda b, g, pt: (0,) * nd)
    page_specs, page_args = [], []
    for c in caches:
        page_specs += _page_specs(c, layer, n_pages, group, 0)
        page_args += [c] * group
    wm = caches[0].shape[2]
    scratch = []
    for w in (wm, LANES, LANES, LANES):
        scratch += [pltpu.VMEM((8, 1), F32), pltpu.VMEM((8, 1), F32), pltpu.VMEM((8, w), F32)]
    gs = pltpu.PrefetchScalarGridSpec(
        num_scalar_prefetch=1,
        grid=(bd, n_pages // group),
        in_specs=page_specs + [bs(a) for a in per_b_args] + [cs(a) for a in consts],
        out_specs=pl.BlockSpec((None, 1, 4 * GROUP_WIDTH), per_b),
        scratch_shapes=scratch,
    )
    return pl.pallas_call(
        functools.partial(_sample_attn_body, group, n_pages, lam_init, qpos),
        out_shape=jax.ShapeDtypeStruct((bd, 1, 4 * GROUP_WIDTH), F32),
        grid_spec=gs,
        compiler_params=_params(("arbitrary", "arbitrary")),
        name="sample_mixers",
    )(pt_flat, *page_args, *per_b_args, *consts)


def _outproj_body(final, o_ref, z_ref, x_ref, mod_ref, w_ref, fn_ref, out_ref):
    d = x_ref.shape[-1]
    y = _dot((o_ref[...] * _silu(z_ref[...])).astype(BF), w_ref[...])
    xn = x_ref[...] + mod_ref[...][:, 2 * d:3 * d] * y
    if final:
        xn = xn * lax.rsqrt(jnp.mean(xn * xn, -1, keepdims=True) + EPS) * fn_ref[...]
    out_ref[...] = xn


def _outproj(o3, z3, x3, mod3, wout, fn, final, tm):
    bk, sk, d = x3.shape
    per_token = mod3.shape[1] != 1
    tok = lambda s, b: (b, s, 0)
    return pl.pallas_call(
        functools.partial(_outproj_body, final),
        out_shape=jax.ShapeDtypeStruct((bk, sk, d), F32),
        grid=(sk // tm, bk),
        in_specs=[pl.BlockSpec((None, tm, o3.shape[2]), tok), pl.BlockSpec((None, tm, z3.shape[2]), tok),
                  pl.BlockSpec((None, tm, d), tok),
                  pl.BlockSpec((None, tm, 3 * d), tok) if per_token
                  else pl.BlockSpec((None, 1, 3 * d), lambda s, b: (b, 0, 0)),
                  _const_spec(wout), _const_spec(fn)],
        out_specs=pl.BlockSpec((None, tm, d), tok),
        compiler_params=_params(("arbitrary", "arbitrary")),
        name="out_proj",
    )(o3, z3, x3, mod3, wout, fn)


def _in_columns():
    widths = (192, 128, 32, 256, 256, 64, 64, 256, 256, 128, 128, 128, 12, 256, 256, 128, 128, 4, 32, 256)
    names = ("mla_cq", "mla_ckv", "mla_kr", "mla_z", "diff_q", "diff_k", "diff_v", "diff_z", "nsa_q", "nsa_cmp",
             "nsa_sel", "nsa_win", "nsa_gate", "nsa_z", "dsa_q", "dsa_kv", "dsa_iq", "dsa_iw", "dsa_ik", "dsa_z")
    off, o = {}, 0
    for n, w in zip(names, widths):
        off[n] = (o, w)
        o += w

    def rng(n):
        return list(range(off[n][0], off[n][0] + off[n][1]))
    pad = lambda k: [-1] * k
    cols = (rng("mla_cq") + pad(64) + rng("mla_ckv")
            + rng("mla_kr") + rng("nsa_gate") + rng("dsa_iw") + pad(LANES - 48)
            + rng("mla_z") + rng("diff_z") + rng("nsa_z") + rng("dsa_z")
            + rng("diff_q") + rng("diff_k") + rng("diff_v") + rng("nsa_q") + rng("nsa_cmp") + rng("nsa_sel")
            + rng("nsa_win") + rng("dsa_q") + rng("dsa_kv") + rng("dsa_iq") + rng("dsa_ik") * 4)
    cols = np.asarray(cols, np.int32)
    assert cols.shape[0] == P_END and o == 2960
    return cols


def _rope_pattern(pos, width, period, rot, active):
    half = rot // 2
    j = np.arange(width)
    jj = j % period
    live = (j < active) & (jj < rot)
    inv = ROPE_THETA ** (-jnp.arange(half, dtype=F32) * 2.0 / rot)
    ang = pos.astype(F32)[:, None] * inv
    cos, sin = jnp.cos(ang), jnp.sin(ang)
    idx = jnp.asarray(jj % half)
    c = jnp.where(jnp.asarray(live), cos[:, idx], 1.0)
    sa = jnp.where(jnp.asarray(live & (jj < half)), -sin[:, idx], 0.0)
    sb = jnp.where(jnp.asarray(live & (jj >= half)), sin[:, idx], 0.0)
    return jnp.stack([c, sa, sb])


def _rope_tables(pos):
    t = [_rope_pattern(pos, 128, 32, 32, 128),
         _rope_pattern(pos, 256, 32, 8, 256),
         _rope_pattern(pos, 128, 32, 8, 64),
         _rope_pattern(pos, 256, 64, 16, 256),
         _rope_pattern(pos, 128, 64, 16, 64),
         _rope_pattern(pos, 128, 32, 32, 32)]
    return jnp.concatenate(t, axis=-1)


def _layer_weights(l, cols, w_ada, b_ada, norm_g, w_in, mla_q_norm, mla_kv_norm, mla_w_uq, mla_w_uk, mla_w_uv,
                   diff_lambda, diff_subln, nsa_cmp_pos, nsa_cmp_w1, nsa_cmp_w2, w_out):
    wl = {}
    wl["w_ada"] = w_ada[l].astype(BF)
    wl["b_ada"] = b_ada[l][None, :]
    wl["g"] = norm_g[l][None, :]
    wl["wr"] = jnp.where(jnp.asarray(cols >= 0)[None, :], w_in[l][:, np.maximum(cols, 0)], 0.0).astype(BF)
    wl["qn"] = jnp.pad(mla_q_norm[l], (0, 256 - MLA_Q_RANK))[None, :]
    wl["kvn"] = mla_kv_norm[l][None, :]
    uq = jnp.pad(mla_w_uq[l], ((0, 256 - MLA_Q_RANK), (0, 0), (0, 0)))
    wl["wuqn"] = uq[:, :, :MLA_NOPE].reshape(256, HEADS * MLA_NOPE).astype(BF)
    wl["wuqp"] = uq[:, :, MLA_NOPE:].reshape(256, HEADS * MLA_ROPE).astype(BF)
    wuk = jnp.zeros((HEADS * MLA_NOPE, HEADS * MLA_KV_RANK), F32)
    wuv = jnp.zeros((HEADS, MLA_KV_RANK, GROUP_WIDTH), F32)
    for h in range(HEADS):
        wuk = wuk.at[h * MLA_NOPE:(h + 1) * MLA_NOPE, h * MLA_KV_RANK:(h + 1) * MLA_KV_RANK].set(mla_w_uk[l][:, h, :].T)
        wuv = wuv.at[h, :, h * HEAD_DIM:(h + 1) * HEAD_DIM].set(mla_w_uv[l][:, h, :])
    wl["wukbd"] = wuk.astype(BF)
    wl["wuv"] = wuv.astype(BF)
    wl["subln"] = jnp.pad(diff_subln[l], (HEAD_DIM, 0))[None, :]
    wl["lam"] = diff_lambda[l]
    w1 = nsa_cmp_w1[l].reshape(2, 2, NSA_CMP_STRIDE, NSA_DK, NSA_DK)
    w1s = jnp.zeros((NSA_CMP_STRIDE, 2, NSA_DK, 2, 2, NSA_DK), F32)
    for c in range(2):
        w1s = w1s.at[:, c, :, :, c, :].set(jnp.transpose(w1[c], (1, 2, 0, 3)))
    wl["w1s"] = w1s.reshape(NSA_CMP_STRIDE, 2 * NSA_DK, 4 * NSA_DK).astype(BF)
    pos = nsa_cmp_pos[l].reshape(2, 2, NSA_CMP_STRIDE, NSA_DK)
    pos8 = jnp.transpose(pos, (2, 1, 0, 3)).reshape(NSA_CMP_STRIDE, 2, 2 * NSA_DK)
    wl["pos8"] = jnp.pad(pos8, ((0, 0), (0, 6), (0, 0)))
    w2 = jnp.zeros((2 * NSA_DK, 2 * NSA_DK), F32)
    for c in range(2):
        w2 = w2.at[c * NSA_DK:(c + 1) * NSA_DK, c * NSA_DK:(c + 1) * NSA_DK].set(nsa_cmp_w2[l][c])
    wl["w2bd"] = w2.astype(BF)
    wl["wout"] = w_out[l].astype(BF)
    return wl


def _overlap(n_cmp, n_rows, n_sel, n_cols):
    i = np.arange(n_rows)[:, None]
    j = np.arange(n_cols)[None, :]
    cs, ss = i * NSA_CMP_STRIDE, j * NSA_SEL_BLOCK
    ov = (cs <= ss + NSA_SEL_BLOCK - 1) & (cs + NSA_CMP_BLOCK - 1 >= ss) & (i < n_cmp) & (j < n_sel)
    return ov.astype(np.float32)


def _round_up(x, m):
    return -(-x // m) * m


def kernel(x_prompt, x_sample, cache_mla, cache_diff, cache_nsa_cmp, cache_nsa_sel, cache_dsa_kv, cache_dsa_idx,
           state_nsa_win, page_table, c_prompt, c_sample, w_ada, b_ada, norm_g, w_in, mla_q_norm, mla_kv_norm,
           mla_w_uq, mla_w_uk, mla_w_uv, diff_lambda, diff_subln, nsa_cmp_pos, nsa_cmp_w1, nsa_cmp_w2, w_out,
           final_norm):
    b, s, d = x_prompt.shape
    bd, s_dec, _ = x_sample.shape
    depth = w_in.shape[0]
    n_pages, page = page_table.shape[1], cache_mla.shape[2]
    past = n_pages * page
    assert s_dec == 1 and page == LANES and state_nsa_win.shape[2] == NSA_WINDOW
    tq = 256 if s % 256 == 0 else 128
    tm_p = 256 if s % 256 == 0 else 128
    assert s % tq == 0 and s >= NSA_WINDOW + tq and bd % 8 == 0 and past % NSA_SEL_BLOCK == 0
    grp = min(32, n_pages)
    n_bucket = max(1, s // 256)
    assert n_pages % grp == 0 and s % (n_bucket * tq) == 0
    cache_mla_t = jnp.swapaxes(cache_mla, 2, 3)
    cache_idx_t = jnp.swapaxes(cache_dsa_idx, 2, 3)

    cols = _in_columns()
    tab_p = _rope_tables(jnp.arange(s))
    tab_s = jnp.broadcast_to(_rope_tables(past + jnp.arange(1)), (3, bd, T_END))
    pt_flat = page_table.reshape(-1)
    fn = final_norm[None, :]

    nch_p = s // NSA_CMP_STRIDE
    ncmp_p = nch_p - NSA_CMP_BLOCK // NSA_CMP_STRIDE + 1
    nsel_p = -(-s // NSA_SEL_BLOCK)
    nsp = _round_up(nsel_p, 8)
    ovt = jnp.asarray(_overlap(ncmp_p, nch_p, nsel_p, nsp).T, BF)
    emat = jnp.asarray((np.arange(s)[None, :] // NSA_SEL_BLOCK == np.arange(nsp)[:, None]).astype(np.float32), BF)
    ntop_p, ksel_p = min(NSA_TOPN, nsel_p), min(DSA_TOPK, s // 4)
    l_s = past + 1
    nch_s = l_s // NSA_CMP_STRIDE
    ncmp_s = nch_s - NSA_CMP_BLOCK // NSA_CMP_STRIDE + 1
    nsel_s = -(-l_s // NSA_SEL_BLOCK)
    nsl = _round_up(nsel_s, LANES)
    ov_s = jnp.asarray(_overlap(ncmp_s, nch_s, nsel_s, nsl), BF)
    ntop_s, ksel_s = min(NSA_TOPN, nsel_s), min(DSA_TOPK, l_s // 4)

    xp = x_prompt
    xs = x_sample.reshape(1, bd, d)
    rows_p, rows_s = [], []
    for l in range(depth):
        wl = _layer_weights(l, cols, w_ada, b_ada, norm_g, w_in, mla_q_norm, mla_kv_norm, mla_w_uq, mla_w_uk,
                            mla_w_uv, diff_lambda, diff_subln, nsa_cmp_pos, nsa_cmp_w1, nsa_cmp_w2, w_out)
        lam_init = 0.8 - 0.6 * math.exp(-0.3 * l)
        final = l == depth - 1
        shared = (wl["g"], wl["wr"], wl["qn"], wl["kvn"], wl["wuqn"], wl["wuqp"], wl["wukbd"])

        mod_p = _ada(c_prompt, wl["w_ada"], wl["b_ada"])[:, None, :]
        pr = _inproj(xp, mod_p, *shared, tab_p, tm_p)
        cmpkv_p = _cmp2(_cmp1_prompt(pr["cmp"], wl["w1s"]), wl["pos8"], wl["w1s"], wl["w2bd"])
        o_p = _prompt_attn(pr, cmpkv_p, wl["wuv"], wl["subln"], wl["lam"], ovt, emat, lam_init,
                           ncmp_p, nsel_p, ntop_p, ksel_p, tq, n_bucket)
        xp = _outproj(o_p, pr["z"], xp, mod_p, wl["wout"], fn, final, tm_p)
        rows_p.append(pr)

        mod_s = _ada(c_sample, wl["w_ada"], wl["b_ada"])[None]
        sr = _inproj(xs, mod_s, *shared, tab_s, bd)
        cmpkv_s = _cmp2(_cmp1_sample(cache_nsa_cmp, l, pt_flat, bd, n_pages, wl["w1s"], grp),
                        wl["pos8"], wl["w1s"], wl["w2bd"])

        def rows8(a):
            t = jnp.transpose(a[:, 0], (1, 0, 2))
            return jnp.pad(t, ((0, 0), (0, 8 - t.shape[1]), (0, 0)))
        iq8 = jnp.stack([sr["iq"][h, 0, :, h * DSA_IDX_DIM:(h + 1) * DSA_IDX_DIM] for h in range(HEADS)], axis=1)
        iq8 = jnp.pad(iq8, ((0, 0), (0, 8 - HEADS), (0, 0)))
        misc_s = sr["misc"][0]
        iw8 = jnp.pad(misc_s[:, M_IW:M_IW + HEADS], ((0, 0), (0, 8 - HEADS)))[:, :, None]
        tok = lambda name: sr[name][0][:, None, :]
        dmask, smask, ocmp = _sample_select(
            cache_idx_t, l, pt_flat, bd, n_pages, iq8, iw8, tok("idx"), rows8(sr["nq"]), cmpkv_s, ov_s,
            ncmp_s, nsel_s, ntop_s, ksel_s, past, grp)
        flat = lambda m: m.reshape(bd, 1, -1)
        per_b = [rows8(sr["qa"]), rows8(sr["dq"]), rows8(sr["nqr"]), rows8(sr["sq"]), misc_s[:, None, :],
                 ocmp, flat(smask), flat(dmask), tok("mla"), tok("diff"), tok("sel"), tok("win"), tok("kv"),
                 state_nsa_win[l]]
        o_s = _sample_attn((cache_mla_t, cache_diff, cache_nsa_sel, cache_dsa_kv), l, pt_flat, bd, n_pages,
                           per_b, [wl["wuv"], wl["subln"], wl["lam"]], lam_init, past, grp)
        xs = _outproj(o_s.reshape(1, bd, 4 * GROUP_WIDTH), sr["z"], xs, mod_s, wl["wout"], fn, final, bd)
        rows_s.append(sr)

    def stack_p(name):
        return jnp.stack([r[name] for r in rows_p])

    def stack_s(name):
        return jnp.stack([r[name][0][:, None, :] for r in rows_s])
    win_p = jnp.stack([r["win"][:, s - min(NSA_WINDOW, s):] for r in rows_p])
    win_s = jnp.stack([jnp.concatenate([state_nsa_win[l][:, 1:], rows_s[l]["win"][0][:, None, :]], axis=1)
                       for l in range(depth)])
    return (xp, xs.reshape(bd, 1, d),
            stack_p("mla"), stack_s("mla"), stack_p("diff"), stack_s("diff"),
            stack_p("cmp"), stack_s("cmp"), stack_p("sel"), stack_s("sel"),
            stack_p("kv"), stack_s("kv"), stack_p("idx"), stack_s("idx"), win_p, win_s)
```

```python
import functools
import math

import numpy as np
import jax
import jax.numpy as jnp
from jax import lax
from jax.experimental import pallas as pl
from jax.experimental.pallas import tpu as pltpu

F32 = jnp.float32
BF = jnp.bfloat16

HEAD_DIM = 64
HEADS = 4
GROUP_WIDTH = HEADS * HEAD_DIM
ROPE_THETA = 500000.0
NEG = -1e30
BIG = 1e9
EPS = 1e-6
MLA_Q_RANK = 192
MLA_KV_RANK = 128
MLA_NOPE = 64
MLA_ROPE = 32
MLA_SCALE = (MLA_NOPE + MLA_ROPE) ** -0.5
DIFF_DK = 32
NSA_DK = 64
NSA_CMP_BLOCK = 32
NSA_CMP_STRIDE = 16
NSA_SEL_BLOCK = 64
NSA_TOPN = 16
NSA_LOCAL = 2
NSA_WINDOW = 512
DSA_DK = 64
DSA_IDX_DIM = 32
DSA_TOPK = 256
DSA_IDX_SCALE = (DSA_IDX_DIM ** -0.5) * (HEADS ** -0.5)
LOG2E = 1.4426950408889634
QS_MLA = MLA_SCALE * LOG2E
QS_DIFF = DIFF_DK ** -0.5 * LOG2E
QS_64 = HEAD_DIM ** -0.5 * LOG2E

LANES = 128
VMEM_LIMIT = 56 * 1024 * 1024
INT_MIN = -2 ** 31

P_CQ, P_CKV, P_MISC, P_Z, P_DQ, P_DIFFKV, P_NQ, P_CMP, P_SEL, P_WIN, P_SQ, P_DSAKV, P_IQ, P_IK4, P_END = (
    0, 256, 384, 512, 1536, 1792, 1920, 2176, 2304, 2432, 2560, 2816, 2944, 3072, 3200)
M_GATE, M_IW = 32, 44
T_QPE, T_H32, T_DIFFKV, T_H64, T_KV64, T_MISC, T_END = 0, 128, 384, 512, 768, 896, 1024


def _dot(a, b):
    return jnp.dot(a, b, preferred_element_type=F32)


def _dot_t(a, b):
    return lax.dot_general(a, b, (((1,), (1,)), ((), ())), preferred_element_type=F32)


def _silu(x):
    return x / (1.0 + jnp.exp(-x))


def _const_spec(a):
    nd = a.ndim
    return pl.BlockSpec(a.shape, lambda *_: (0,) * nd)


def _params(sem):
    return pltpu.CompilerParams(dimension_semantics=sem, vmem_limit_bytes=VMEM_LIMIT)


def _ada_body(c_ref, w_ref, b_ref, o_ref):
    o_ref[...] = _dot(c_ref[...].astype(BF), w_ref[...]) + b_ref[...]


def _ada(c, w_bf, b):
    return pl.pallas_call(
        _ada_body,
        out_shape=jax.ShapeDtypeStruct((c.shape[0], w_bf.shape[1]), F32),
        compiler_params=pltpu.CompilerParams(vmem_limit_bytes=VMEM_LIMIT),
        name="ada_mod",
    )(c, w_bf, b)


def _rope(x, tab_ref, off, half):
    w = x.shape[1]
    c = tab_ref[0, :, off:off + w]
    sa = tab_ref[1, :, off:off + w]
    sb = tab_ref[2, :, off:off + w]
    return x * c + pltpu.roll(x, w - half, 1) * sa + pltpu.roll(x, half, 1) * sb


def _inproj_body(x_ref, mod_ref, g_ref, w_ref, qn_ref, kvn_ref, wuqn_ref, wuqp_ref, wuk_ref, tab_ref,
                 qa_ref, dq_ref, nq_ref, nqr_ref, sq_ref, iq_ref, misc_ref, z_ref,
                 mla_ref, diff_ref, cmp_ref, sel_ref, win_ref, kv_ref, idx_ref,
                 mlab_ref, diffb_ref, selb_ref, winb_ref, kvb_ref, ik4b_ref):
    d = x_ref.shape[-1]
    x = x_ref[...]
    mod = mod_ref[...]
    shift, scale = mod[:, 0:d], mod[:, d:2 * d]
    xn = x * lax.rsqrt(jnp.mean(x * x, -1, keepdims=True) + EPS) * g_ref[...]
    hb = (xn * (1.0 + scale) + shift).astype(BF)
    lane = lax.broadcasted_iota(jnp.int32, (x.shape[0], LANES), 1)

    def proj(a, b):
        return _dot(hb, w_ref[:, a:b])

    cq = proj(P_CQ, P_CKV)
    cqn = (cq * lax.rsqrt(jnp.sum(cq * cq, -1, keepdims=True) * (1.0 / MLA_Q_RANK) + EPS) * qn_ref[...]).astype(BF)
    q_lat = _dot(_dot(cqn, wuqn_ref[...]).astype(BF), wuk_ref[...])
    q_pe = _rope(_dot(cqn, wuqp_ref[...]), tab_ref, T_QPE, MLA_ROPE // 2)
    for h in range(HEADS):
        qa_ref[h, :, 0:MLA_KV_RANK] = (q_lat[:, h * LANES:(h + 1) * LANES] * QS_MLA).astype(BF)
        qa_ref[h, :, MLA_KV_RANK:] = (q_pe[:, h * MLA_ROPE:(h + 1) * MLA_ROPE] * QS_MLA).astype(BF)
    ckv = proj(P_CKV, P_MISC)
    c_new = ckv * lax.rsqrt(jnp.mean(ckv * ckv, -1, keepdims=True) + EPS) * kvn_ref[...]
    misc = proj(P_MISC, P_Z)
    misc_r = _rope(misc, tab_ref, T_MISC, MLA_ROPE // 2)
    mla_ref[:, 0:MLA_KV_RANK] = c_new
    mla_ref[:, MLA_KV_RANK:] = misc_r[:, 0:MLA_ROPE]
    mlab_ref[:, 0:MLA_KV_RANK] = c_new.astype(BF)
    mlab_ref[:, MLA_KV_RANK:] = misc_r[:, 0:MLA_ROPE].astype(BF)
    is_gate = (lane >= M_GATE) & (lane < M_GATE + 3 * HEADS)
    misc_ref[...] = jnp.where(is_gate, 1.0 / (1.0 + jnp.exp(-misc)), misc_r)
    z_ref[...] = proj(P_Z, P_DQ)

    dq = _rope(proj(P_DQ, P_DIFFKV), tab_ref, T_H32, DIFF_DK // 8) * QS_DIFF
    for h in range(HEADS):
        slab = dq[:, (h // 2) * LANES:(h // 2 + 1) * LANES]
        if h % 2:
            slab = pltpu.roll(slab, HEAD_DIM, 1)
        dq_ref[h] = jnp.where(lane < DIFF_DK, slab, 0.0).astype(BF)
        dq_ref[HEADS + h] = jnp.where((lane >= DIFF_DK) & (lane < 2 * DIFF_DK), slab, 0.0).astype(BF)
    diff_row = _rope(proj(P_DIFFKV, P_NQ), tab_ref, T_DIFFKV, DIFF_DK // 8)
    diff_ref[...] = diff_row
    diffb_ref[...] = diff_row.astype(BF)

    def per_head(v, out_ref):
        for h in range(HEADS):
            slab = v[:, (h // 2) * LANES:(h // 2 + 1) * LANES]
            if h % 2:
                slab = pltpu.roll(slab, HEAD_DIM, 1)
            out_ref[h] = jnp.where(lane < HEAD_DIM, slab, 0.0).astype(BF)

    nq = proj(P_NQ, P_CMP) * QS_64
    per_head(nq, nq_ref)
    per_head(_rope(nq, tab_ref, T_H64, NSA_DK // 8), nqr_ref)
    cmp_ref[...] = proj(P_CMP, P_SEL)
    sel_row = _rope(proj(P_SEL, P_WIN), tab_ref, T_KV64, NSA_DK // 8)
    sel_ref[...] = sel_row
    selb_ref[...] = sel_row.astype(BF)
    win_row = _rope(proj(P_WIN, P_SQ), tab_ref, T_KV64, NSA_DK // 8)
    win_ref[...] = win_row
    winb_ref[...] = win_row.astype(BF)

    per_head(_rope(proj(P_SQ, P_DSAKV) * QS_64, tab_ref, T_H64, DSA_DK // 8), sq_ref)
    kv_row = _rope(proj(P_DSAKV, P_IQ), tab_ref, T_KV64, DSA_DK // 8)
    kv_ref[...] = kv_row
    kvb_ref[...] = kv_row.astype(BF)
    iq = _rope(proj(P_IQ, P_IK4), tab_ref, T_H32, DSA_IDX_DIM // 8)
    for h in range(HEADS):
        live = (lane >= h * DSA_IDX_DIM) & (lane < (h + 1) * DSA_IDX_DIM)
        iq_ref[h] = jnp.where(live, iq, 0.0).astype(BF)
    ik4 = _rope(proj(P_IK4, P_END), tab_ref, T_H32, DSA_IDX_DIM // 8)
    idx_ref[...] = ik4[:, 0:DSA_IDX_DIM]
    ik4b_ref[...] = ik4.astype(BF)


def _inproj(x3, mod3, g, wr, qn, kvn, wuqn, wuqp, wukbd, tab, tm):
    bk, sk, d = x3.shape
    per_token = mod3.shape[1] != 1
    grid = (sk // tm, bk)
    tok = lambda s, b: (b, s, 0)
    hed = lambda s, b: (0, b, s, 0)
    in_specs = [
        pl.BlockSpec((None, tm, d), tok),
        pl.BlockSpec((None, tm, 3 * d), tok) if per_token else pl.BlockSpec((None, 1, 3 * d), lambda s, b: (b, 0, 0)),
        _const_spec(g), _const_spec(wr), _const_spec(qn), _const_spec(kvn),
        _const_spec(wuqn), _const_spec(wuqp), _const_spec(wukbd),
        pl.BlockSpec((3, tm, T_END), lambda s, b: (0, s, 0)),
    ]

    def hs(n, w):
        return jax.ShapeDtypeStruct((n, bk, sk, w), BF), pl.BlockSpec((n, None, tm, w), hed)

    def ts(w, dt):
        return jax.ShapeDtypeStruct((bk, sk, w), dt), pl.BlockSpec((None, tm, w), tok)

    outs = [hs(HEADS, MLA_KV_RANK + MLA_ROPE), hs(2 * HEADS, LANES), hs(HEADS, LANES), hs(HEADS, LANES),
            hs(HEADS, LANES), hs(HEADS, LANES), ts(LANES, F32), ts(4 * GROUP_WIDTH, F32),
            ts(MLA_KV_RANK + MLA_ROPE, F32), ts(LANES, F32), ts(LANES, F32), ts(LANES, F32), ts(LANES, F32),
            ts(LANES, F32), ts(DSA_IDX_DIM, F32),
            ts(MLA_KV_RANK + MLA_ROPE, BF), ts(LANES, BF), ts(LANES, BF), ts(LANES, BF), ts(LANES, BF),
            ts(LANES, BF)]
    names = ("qa", "dq", "nq", "nqr", "sq", "iq", "misc", "z", "mla", "diff", "cmp", "sel", "win", "kv", "idx",
             "mla_b", "diff_b", "sel_b", "win_b", "kv_b", "ik4_b")
    res = pl.pallas_call(
        _inproj_body,
        out_shape=[o[0] for o in outs],
        grid=grid,
        in_specs=in_specs,
        out_specs=[o[1] for o in outs],
        compiler_params=_params(("arbitrary", "arbitrary")),
        name="in_proj",
    )(x3, mod3, g, wr, qn, kvn, wuqn, wuqp, wukbd, tab)
    return dict(zip(names, res))


def _chunk_proj(row_refs, w1s_ref):
    acc = None
    for s in range(NSA_CMP_STRIDE):
        parts = [r[pl.ds(s, r.shape[0] // NSA_CMP_STRIDE, stride=NSA_CMP_STRIDE), :] for r in row_refs]
        xs = parts[0] if len(parts) == 1 else jnp.concatenate(parts, axis=0)
        t = _dot(xs.astype(BF), w1s_ref[s])
        acc = t if acc is None else acc + t
    return acc


def _cmp1_prompt_body(rows_ref, w1s_ref, u_ref):
    u_ref[...] = _chunk_proj([rows_ref], w1s_ref)


def _cmp1_prompt(cmp_rows, w1s):
    b, s, _ = cmp_rows.shape
    return pl.pallas_call(
        _cmp1_prompt_body,
        out_shape=jax.ShapeDtypeStruct((b, s // NSA_CMP_STRIDE, 2 * LANES), F32),
        grid=(b,),
        in_specs=[pl.BlockSpec((None, s, LANES), lambda i: (i, 0, 0)), _const_spec(w1s)],
        out_specs=pl.BlockSpec((None, s // NSA_CMP_STRIDE, 2 * LANES), lambda i: (i, 0, 0)),
        compiler_params=_params(("arbitrary",)),
        name="nsa_cmp1_prompt",
    )(cmp_rows, w1s)


def _page_specs(cache, layer, n_pages, group, n_extra):
    def make(k):
        def imap(b, g, pt):
            return (layer, pt[b * n_pages + g * group + k], 0, 0)
        return pl.BlockSpec((None, None) + cache.shape[2:], imap)
    return [make(k) for k in range(group)]


def _cmp1_sample_body(group, pt_ref, *refs):
    pages, w1s_ref, u_ref = refs[:group], refs[group], refs[group + 1]
    u_ref[...] = _chunk_proj(list(pages), w1s_ref)


def _cmp1_sample(cache, layer, pt_flat, bd, n_pages, w1s, group):
    page = cache.shape[2]
    cpp = page // NSA_CMP_STRIDE
    gs = pltpu.PrefetchScalarGridSpec(
        num_scalar_prefetch=1,
        grid=(bd, n_pages // group),
        in_specs=_page_specs(cache, layer, n_pages, group, 0) + [pl.BlockSpec(w1s.shape, lambda b, g, pt: (0, 0, 0))],
        out_specs=pl.BlockSpec((None, group * cpp, 2 * LANES), lambda b, g, pt: (b, g, 0)),
    )
    return pl.pallas_call(
        functools.partial(_cmp1_sample_body, group),
        out_shape=jax.ShapeDtypeStruct((bd, n_pages * cpp, 2 * LANES), F32),
        grid_spec=gs,
        compiler_params=_params(("arbitrary", "arbitrary")),
        name="nsa_cmp1_sample",
    )(pt_flat, *([cache] * group), w1s)


def _cmp2_body(u_ref, pos_ref, w1s_ref, w2_ref, o_ref):
    n = u_ref.shape[0]
    pacc = None
    for s in range(NSA_CMP_STRIDE):
        t = _dot(pos_ref[s].astype(BF), w1s_ref[s])
        pacc = t if pacc is None else pacc + t
    posterm = pacc[0:1, 0:LANES] + pacc[1:2, LANES:2 * LANES]
    u = u_ref[...]
    pre = u[:, 0:LANES] + pltpu.roll(u[:, LANES:2 * LANES], n - 1, 0) + posterm
    o_ref[...] = _dot(_silu(pre).astype(BF), w2_ref[...])


def _cmp2(u, pos8, w1s, w2bd):
    b, n, _ = u.shape
    return pl.pallas_call(
        _cmp2_body,
        out_shape=jax.ShapeDtypeStruct((b, n, LANES), F32),
        grid=(b,),
        in_specs=[pl.BlockSpec((None, n, 2 * LANES), lambda i: (i, 0, 0)),
                  _const_spec(pos8), _const_spec(w1s), _const_spec(w2bd)],
        out_specs=pl.BlockSpec((None, n, LANES), lambda i: (i, 0, 0)),
        compiler_params=_params(("arbitrary",)),
        name="nsa_cmp2",
    )(u, pos8, w1s, w2bd)


def _softmax_parts(s, mask):
    s = jnp.where(mask, s, NEG)
    m = jnp.max(s, -1, keepdims=True)
    p = jnp.where(mask, jnp.exp2(s - m), 0.0)
    return p, 1.0 / jnp.maximum(jnp.sum(p, -1, keepdims=True), 1e-30)


def _softmax_bias(s, bias):
    s = s + bias
    p = jnp.exp2(s - jnp.max(s, -1, keepdims=True))
    return p, 1.0 / jnp.maximum(jnp.sum(p, -1, keepdims=True), 1e-30)


def _lambda(lam_ref, lam_init):
    lp = lam_ref[...]
    a = jnp.sum(lp[0:1] * lp[1:2], -1, keepdims=True)
    b = jnp.sum(lp[2:3] * lp[3:4], -1, keepdims=True)
    return jnp.exp(a) - jnp.exp(b) + lam_init


def _sort_key(x):
    bits = lax.bitcast_convert_type(x, jnp.int32)
    return jnp.where(bits < 0, bits ^ jnp.int32(0x7FFFFFFF), bits)


def _kth_largest_key(key, k, count):
    rows = count(key >= 0)
    t = jnp.where(rows >= k, jnp.int32(0), jnp.int32(INT_MIN))

    def step(i, t):
        cand = t + jnp.left_shift(jnp.int32(1), 30 - i)
        return jnp.where(count(key >= cand) >= k, cand, t)
    return lax.fori_loop(0, 31, step, t)


def _wrap32(v):
    return jnp.int32((v + 2 ** 31) % 2 ** 32 - 2 ** 31)


def _kth_largest_key_nibbles(key, k, count):
    t = jnp.full((1, 1), INT_MIN, jnp.int32)
    for shift in range(28, -1, -4):
        n_ok = jnp.zeros((1, 1), jnp.int32)
        for c in range(1, 16):
            ok = count(key >= t + _wrap32(c << shift)) >= k
            n_ok = n_ok + jnp.where(ok, 1, 0)
        t = t + n_ok * _wrap32(1 << shift)
    return t


def _tri_consts():
    r = lax.broadcasted_iota(jnp.int32, (LANES, LANES), 0)
    c = lax.broadcasted_iota(jnp.int32, (LANES, LANES), 1)
    return jnp.where(r < c, 1.0, 0.0).astype(BF), jnp.ones((LANES, LANES), BF)


def _place_heads(parts, lane):
    slabs = []
    for m in range(HEADS // 2):
        slabs.append(jnp.where(lane < HEAD_DIM, pltpu.roll(parts[2 * m], HEAD_DIM, 1), parts[2 * m + 1]))
    return jnp.concatenate(slabs, axis=1)


def _prompt_attn_body(q_tile0, n_cmp, n_sel, n_top, k_sel, lam_init,
                      qa_ref, dq_ref, nq_ref, nqr_ref, sq_ref, iq_ref, misc_ref,
                      mla_ref, diff_ref, cmpkv_ref, sel_ref, win_ref, kv_ref, ik4_ref,
                      wuv_ref, subln_ref, lam_ref, ovt_ref, e_ref, o_ref):
    tq = misc_ref.shape[0]
    s_len = mla_ref.shape[0]
    nsp = ovt_ref.shape[0]
    q0 = (pl.program_id(1) + q_tile0) * tq
    qpos = q0 + lax.broadcasted_iota(jnp.int32, (tq, 1), 0)
    kpos = lax.broadcasted_iota(jnp.int32, (1, s_len), 1)
    cbias = jnp.where(kpos <= qpos, 0.0, NEG)
    lane = lax.broadcasted_iota(jnp.int32, (tq, LANES), 1)
    misc = misc_ref[...]
    lam = _lambda(lam_ref, lam_init)

    kc = mla_ref[:, 0:MLA_KV_RANK]
    o_a = None
    for h in range(HEADS):
        p, r = _softmax_bias(_dot_t(qa_ref[h], mla_ref[...]), cbias)
        o_lat = _dot(p.astype(BF), kc) * r
        t = _dot(o_lat.astype(BF), wuv_ref[h])
        o_a = t if o_a is None else o_a + t
    o_ref[:, 0:GROUP_WIDTH] = o_a

    kd = diff_ref[...]
    parts = []
    for h in range(HEADS):
        p1, r1 = _softmax_bias(_dot_t(dq_ref[h], kd), cbias)
        p2, r2 = _softmax_bias(_dot_t(dq_ref[HEADS + h], kd), cbias)
        o = _dot(p1.astype(BF), kd) * r1 - lam * (_dot(p2.astype(BF), kd) * r2)
        ms = jnp.sum(jnp.where(lane >= HEAD_DIM, o * o, 0.0), -1, keepdims=True) * (1.0 / HEAD_DIM)
        parts.append(o * lax.rsqrt(ms + EPS) * subln_ref[...] * (1.0 - lam_init))
    o_ref[:, GROUP_WIDTH:2 * GROUP_WIDTH] = _place_heads(parts, lane)

    ckv = cmpkv_ref[...].astype(BF)
    ncol = lax.broadcasted_iota(jnp.int32, (1, ckv.shape[0]), 1)
    cmask = (ncol * NSA_CMP_STRIDE + (NSA_CMP_BLOCK - 1) <= qpos) & (ncol < n_cmp)
    o_cmp, pcsum = [], None
    for h in range(HEADS):
        p, r = _softmax_parts(_dot_t(nq_ref[h], ckv), cmask)
        pc = p * r
        o_cmp.append(_dot(pc.astype(BF), ckv))
        pcsum = pc if pcsum is None else pcsum + pc
    hi = pcsum.astype(BF)
    lo = (pcsum - hi.astype(F32)).astype(BF)
    imp_t = _dot_t(ovt_ref[...], hi) + _dot_t(ovt_ref[...], lo)
    jb = lax.broadcasted_iota(jnp.int32, (nsp, tq), 0)
    qblk = (q0 + lax.broadcasted_iota(jnp.int32, (1, tq), 1)) // NSA_SEL_BLOCK
    forced = (jb == 0) | (jb > qblk - NSA_LOCAL)
    score = jnp.where(jb > qblk, NEG, jnp.where(forced, BIG, imp_t))
    score = jnp.where(jb < n_sel, score, -jnp.inf)
    rank = jnp.zeros((nsp, tq), F32)
    for j2 in range(n_sel):
        other = score[j2:j2 + 1, :]
        tie = jnp.where(jb > j2, 1.0, 0.0)
        rank = rank + jnp.where(other > score, 1.0, jnp.where(other == score, tie, 0.0))
    sel_t = jnp.where((rank < n_top) & (jb < n_sel), 1.0, 0.0)
    sel_keys = lax.dot_general(sel_t.astype(BF), e_ref[...], (((0,), (0,)), ((), ())),
                               preferred_element_type=F32)
    sbias = jnp.where(sel_keys > 0.5, cbias, NEG)
    ks = sel_ref[...]
    o_sel = []
    for h in range(HEADS):
        p, r = _softmax_bias(_dot_t(nqr_ref[h], ks), sbias)
        o_sel.append(_dot(p.astype(BF), ks) * r)
    span = min(NSA_WINDOW + tq, s_len)
    start = pl.multiple_of(jnp.maximum(q0 - NSA_WINDOW, 0), tq)
    kw = win_ref[pl.ds(start, span), :]
    wpos = start + lax.broadcasted_iota(jnp.int32, (1, span), 1)
    wbias = jnp.where((wpos <= qpos) & (wpos > qpos - NSA_WINDOW), 0.0, NEG)
    parts = []
    for h in range(HEADS):
        p, r = _softmax_bias(_dot_t(nqr_ref[h], kw), wbias)
        o_win = _dot(p.astype(BF), kw) * r
        g = [misc[:, M_GATE + HEADS * i + h:M_GATE + HEADS * i + h + 1] for i in range(3)]
        parts.append(g[0] * o_cmp[h] + g[1] * o_sel[h] + g[2] * o_win)
    o_ref[:, 2 * GROUP_WIDTH:3 * GROUP_WIDTH] = _place_heads(parts, lane)

    ki = ik4_ref[...]
    isc = None
    for h in range(HEADS):
        t = jnp.maximum(_dot_t(iq_ref[h], ki), 0.0) * misc[:, M_IW + h:M_IW + h + 1]
        isc = t if isc is None else isc + t
    isc = jnp.where(kpos <= qpos, isc * DSA_IDX_SCALE + 0.0, NEG)
    key = _sort_key(isc)
    count = lambda pred: jnp.sum(jnp.where(pred, 1.0, 0.0), -1, keepdims=True)
    kk = float(min(k_sel, s_len))
    kth = _kth_largest_key(key, kk, count)
    need = kk - count(key > kth)
    tri, ones = _tri_consts()
    offs = jnp.zeros((tq, LANES), F32)
    chunks = []
    for c in range(s_len // LANES):
        kc_ = key[:, c * LANES:(c + 1) * LANES]
        eq = jnp.where(kc_ == kth, 1.0, 0.0).astype(BF)
        before = _dot(eq, tri) + offs
        offs = offs + _dot(eq, ones)
        keep = (kc_ > kth) | ((kc_ == kth) & (before < need))
        chunks.append(jnp.where(keep, cbias[:, c * LANES:(c + 1) * LANES], NEG))
    dbias = jnp.concatenate(chunks, axis=1)
    kv = kv_ref[...]
    parts = []
    for h in range(HEADS):
        p, r = _softmax_bias(_dot_t(sq_ref[h], kv), dbias)
        parts.append(_dot(p.astype(BF), kv) * r)
    o_ref[:, 3 * GROUP_WIDTH:] = _place_heads(parts, lane)


def _prompt_attn(pr, cmpkv, wuv, subln, lam_p, ovt, emat, lam_init, n_cmp, n_sel, n_top, k_sel, tq, n_bucket):
    _, b, s, _ = pr["qa"].shape
    qb = s // n_bucket
    seq = lambda i, j: (i, 0, 0)
    qnames = ("qa", "dq", "nq", "nqr", "sq", "iq")
    knames = ("mla_b", "diff_b", None, "sel_b", "win_b", "kv_b", "ik4_b")
    kargs = [cmpkv if n is None else pr[n] for n in knames]
    outs = []
    for bk in range(n_bucket):
        t0, kv_len = bk * (qb // tq), (bk + 1) * qb

        def hs(a, t0=t0):
            return pl.BlockSpec((a.shape[0], None, tq, a.shape[3]), lambda i, j: (0, i, j + t0, 0))

        def ks(a, name, kv_len=kv_len):
            return pl.BlockSpec((None, a.shape[1] if name is None else kv_len, a.shape[2]), seq)
        consts = [wuv, subln, lam_p, ovt, emat[:, :kv_len]]
        outs.append(pl.pallas_call(
            functools.partial(_prompt_attn_body, t0, n_cmp, n_sel, n_top, k_sel, lam_init),
            out_shape=jax.ShapeDtypeStruct((b, qb, 4 * GROUP_WIDTH), F32),
            grid=(b, qb // tq),
            in_specs=[hs(pr[n]) for n in qnames]
            + [pl.BlockSpec((None, tq, LANES), lambda i, j, t0=t0: (i, j + t0, 0))]
            + [ks(a, n) for a, n in zip(kargs, knames)] + [_const_spec(c) for c in consts],
            out_specs=pl.BlockSpec((None, tq, 4 * GROUP_WIDTH), lambda i, j: (i, j, 0)),
            compiler_params=_params(("arbitrary", "arbitrary")),
            name="prompt_mixers",
        )(*[pr[n] for n in qnames], pr["misc"], *kargs, *consts))
    return outs[0] if n_bucket == 1 else jnp.concatenate(outs, axis=1)


def _sample_select_body(group, n_pages, n_cmp, n_sel, n_top, k_sel, qpos, pt_ref, *refs):
    pages = refs[:group]
    (iq_ref, iw_ref, ikn_ref, nq_ref, cmpkv_ref, ov_ref,
     dmask_ref, smask_ref, ocmp_ref, isc_ref) = refs[group:]
    g = pl.program_id(1)
    page = pages[0].shape[1]
    rp = isc_ref.shape[0]
    q = iq_ref[...]
    w = iw_ref[...]
    sub = lax.broadcasted_iota(jnp.int32, (group, page), 0)
    rows = jnp.zeros((group, page), F32)
    kb = jnp.concatenate([pg[...].astype(BF) for pg in pages], axis=1)
    sc = jnp.sum(jnp.maximum(_dot(q, kb), 0.0) * w, 0, keepdims=True) * DSA_IDX_SCALE + 0.0
    for k in range(group):
        rows = jnp.where(sub == k, sc[:, k * page:(k + 1) * page], rows)
    isc_ref[pl.ds(pl.multiple_of(g * group, group), group), :] = rows

    @pl.when(g == pl.num_programs(1) - 1)
    def _():
        subt = lax.broadcasted_iota(jnp.int32, (rp - n_pages, page), 0)
        lanet = lax.broadcasted_iota(jnp.int32, (rp - n_pages, page), 1)
        kn = ikn_ref[...].astype(BF).astype(F32)
        sn = jnp.maximum(jnp.sum(q.astype(F32) * kn, -1, keepdims=True), 0.0) * w
        rn = jnp.sum(sn, 0, keepdims=True) * DSA_IDX_SCALE + 0.0
        isc_ref[n_pages:rp, :] = jnp.where((subt == 0) & (lanet == 0), rn, -jnp.inf)
        key = _sort_key(isc_ref[...])
        live = key[0:n_pages + 8]
        count = lambda pred: jnp.sum(jnp.sum(jnp.where(pred, 1.0, 0.0), 0, keepdims=True), 1, keepdims=True)
        kth = _kth_largest_key_nibbles(live, float(k_sel), count)
        need = float(k_sel) - count(live > kth)
        tri, ones = _tri_consts()
        eq = jnp.where(key == kth, 1.0, 0.0).astype(BF)
        rr = lax.broadcasted_iota(jnp.int32, (rp, rp), 0)
        cc = lax.broadcasted_iota(jnp.int32, (rp, rp), 1)
        rows_before = _dot(jnp.where(cc < rr, 1.0, 0.0).astype(BF), _dot(eq, ones).astype(BF))
        before = _dot(eq, tri) + rows_before
        dmask_ref[...] = jnp.where((key > kth) | ((key == kth) & (before < need)), 1.0, 0.0)
        ckv = cmpkv_ref[...].astype(BF)
        ncol = lax.broadcasted_iota(jnp.int32, (1, ckv.shape[0]), 1)
        cmask = (ncol * NSA_CMP_STRIDE + (NSA_CMP_BLOCK - 1) <= qpos) & (ncol < n_cmp)
        p, r = _softmax_parts(_dot_t(nq_ref[...], ckv), cmask)
        pc = p * r
        ocmp_ref[...] = _dot(pc.astype(BF), ckv)
        head_rows = lax.broadcasted_iota(jnp.int32, pc.shape, 0) < HEADS
        pcs = jnp.where(head_rows, pc, 0.0)
        hi = pcs.astype(BF)
        lo = (pcs - hi.astype(F32)).astype(BF)
        imp = jnp.sum(_dot(hi, ov_ref[...]) + _dot(lo, ov_ref[...]), 0, keepdims=True)
        nsl = imp.shape[1]
        jb = lax.broadcasted_iota(jnp.int32, (1, nsl), 1)
        qblk = qpos // NSA_SEL_BLOCK
        forced = (jb == 0) | (jb > qblk - NSA_LOCAL)
        score = jnp.where(jb > qblk, NEG, jnp.where(forced, BIG, imp))
        score = jnp.where(jb < n_sel, score, -jnp.inf)
        sel = jnp.zeros((1, nsl), F32)
        for _ in range(n_top):
            m = jnp.max(score, 1, keepdims=True)
            first = jnp.min(jnp.where(score == m, jb, nsl), 1, keepdims=True)
            hit = jb == first
            sel = jnp.where(hit, 1.0, sel)
            score = jnp.where(hit, -jnp.inf, score)
        bpp = page // NSA_SEL_BLOCK
        pr_ = lax.broadcasted_iota(jnp.int32, (rp, nsl), 0)
        pj = lax.broadcasted_iota(jnp.int32, (rp, nsl), 1)
        a = jnp.where(pj // bpp == pr_, sel, 0.0).astype(BF)
        fj = lax.broadcasted_iota(jnp.int32, (nsl, page), 0)
        fc = lax.broadcasted_iota(jnp.int32, (nsl, page), 1)
        f = jnp.where(fj % bpp == fc // NSA_SEL_BLOCK, 1.0, 0.0).astype(BF)
        smask_ref[...] = _dot(a, f)


def _sample_select(cache_idx, layer, pt_flat, bd, n_pages, iq8, iw8, ik_new, nq8, cmpkv, ov,
                   n_cmp, n_sel, n_top, k_sel, qpos, group):
    page = cache_idx.shape[3]
    rp = _round_up(n_pages + 1, LANES)
    per_b = lambda b, g, pt: (b, 0, 0)

    def bs(a):
        return pl.BlockSpec((None,) + a.shape[1:], per_b)
    gs = pltpu.PrefetchScalarGridSpec(
        num_scalar_prefetch=1,
        grid=(bd, n_pages // group),
        in_specs=_page_specs(cache_idx, layer, n_pages, group, 0)
        + [bs(iq8), bs(iw8), bs(ik_new), bs(nq8), bs(cmpkv), pl.BlockSpec(ov.shape, lambda b, g, pt: (0, 0))],
        out_specs=[pl.BlockSpec((None, rp, page), per_b), pl.BlockSpec((None, rp, page), per_b),
                   pl.BlockSpec((None, 8, LANES), per_b)],
        scratch_shapes=[pltpu.VMEM((rp, page), F32)],
    )
    return pl.pallas_call(
        functools.partial(_sample_select_body, group, n_pages, n_cmp, n_sel, n_top, k_sel, qpos),
        out_shape=[jax.ShapeDtypeStruct((bd, rp, page), F32), jax.ShapeDtypeStruct((bd, rp, page), F32),
                   jax.ShapeDtypeStruct((bd, 8, LANES), F32)],
        grid_spec=gs,
        compiler_params=_params(("arbitrary", "arbitrary")),
        name="sample_select",
    )(pt_flat, *([cache_idx] * group), iq8, iw8, ik_new, nq8, cmpkv, ov)


def _online_update(q, pages, transposed, mask, m_ref, l_ref, acc_ref):
    kb = jnp.concatenate(pages, axis=1 if transposed else 0)
    s = _dot(q, kb) if transposed else _dot_t(q, kb)
    if mask is not None:
        mk = mask > 0.5
        s = jnp.where(mk, s, NEG)
    m_old = m_ref[...]
    m_new = jnp.maximum(m_old, jnp.max(s, -1, keepdims=True))
    alpha = jnp.exp2(m_old - m_new)
    p = jnp.exp2(s - m_new)
    if mask is not None:
        p = jnp.where(mk, p, 0.0)
    l_ref[...] = alpha * l_ref[...] + jnp.sum(p, -1, keepdims=True)
    pb = p.astype(BF)
    pv = _dot_t(pb, kb) if transposed else _dot(pb, kb)
    acc_ref[...] = alpha * acc_ref[...] + pv
    m_ref[...] = m_new


def _merge_new_key(q, row, valid, m, l, acc):
    rb = row.astype(BF).astype(F32)
    s = jnp.sum(q.astype(F32) * rb, -1, keepdims=True)
    s = jnp.where(valid, s, NEG)
    m_new = jnp.maximum(m, s)
    alpha = jnp.exp2(m - m_new)
    p = jnp.where(valid, jnp.exp2(s - m_new), 0.0)
    l_new = alpha * l + p
    acc_new = alpha * acc + p.astype(BF).astype(F32) * rb
    return acc_new / jnp.maximum(l_new, 1e-30)


def _sample_attn_body(group, n_pages, lam_init, qpos, pt_ref, *refs):
    mla_pg, diff_pg = refs[0:group], refs[group:2 * group]
    sel_pg, kv_pg = refs[2 * group:3 * group], refs[3 * group:4 * group]
    (qa_ref, dq_ref, nqr_ref, sq_ref, misc_ref, ocmp_ref, smask_ref, dmask_ref,
     mlan_ref, diffn_ref, seln_ref, winn_ref, kvn_ref, wbuf_ref,
     wuv_ref, subln_ref, lam_ref, o_ref,
     ma, la, acca, mb, lb, accb, mc, lc, accc, md, ld, accd) = refs[4 * group:]
    g = pl.program_id(1)
    states = ((ma, la, acca), (mb, lb, accb), (mc, lc, accc), (md, ld, accd))

    @pl.when(g == 0)
    def _():
        for m_ref, l_ref, acc_ref in states:
            m_ref[...] = jnp.full(m_ref.shape, NEG, F32)
            l_ref[...] = jnp.zeros(l_ref.shape, F32)
            acc_ref[...] = jnp.zeros(acc_ref.shape, F32)

    def bf_pages(pgs):
        return [p[...].astype(BF) for p in pgs]

    def group_mask(mref):
        return mref[:, pl.ds(pl.multiple_of(g * (group * LANES), group * LANES), group * LANES)]

    _online_update(qa_ref[...], bf_pages(mla_pg), True, None, ma, la, acca)
    _online_update(dq_ref[...], bf_pages(diff_pg), False, None, mb, lb, accb)
    _online_update(nqr_ref[...], bf_pages(sel_pg), False, group_mask(smask_ref), mc, lc, accc)
    _online_update(sq_ref[...], bf_pages(kv_pg), False, group_mask(dmask_ref), md, ld, accd)

    @pl.when(g == pl.num_programs(1) - 1)
    def _():
        lane = lax.broadcasted_iota(jnp.int32, (8, LANES), 1)
        rowi = lax.broadcasted_iota(jnp.int32, (8, LANES), 0)
        misc = misc_ref[...]
        lam = _lambda(lam_ref, lam_init)
        true8 = jnp.full((8, 1), True)
        oa = _merge_new_key(qa_ref[...], mlan_ref[...], true8, ma[...], la[...], acca[...])
        o_lat = oa[:, 0:MLA_KV_RANK]
        o_a = None
        for h in range(HEADS):
            t = _dot(jnp.where(rowi == h, o_lat, 0.0).astype(BF), wuv_ref[h])
            o_a = t if o_a is None else o_a + t
        o_ref[:, 0:GROUP_WIDTH] = jnp.sum(o_a, 0, keepdims=True)
        ob = _merge_new_key(dq_ref[...], diffn_ref[...], true8, mb[...], lb[...], accb[...])
        o = ob - lam * pltpu.roll(ob, HEADS, 0)
        ms = jnp.sum(jnp.where(lane >= HEAD_DIM, o * o, 0.0), -1, keepdims=True) * (1.0 / HEAD_DIM)
        o = o * lax.rsqrt(ms + EPS) * subln_ref[...] * (1.0 - lam_init)
        o_ref[:, GROUP_WIDTH:2 * GROUP_WIDTH] = _place_heads([o[h:h + 1] for h in range(HEADS)], lane[0:1])
        new_at = n_pages * LANES
        snew = smask_ref[:, new_at:new_at + 1] > 0.5
        osel = _merge_new_key(nqr_ref[...], seln_ref[...], snew, mc[...], lc[...], accc[...])
        wb = wbuf_ref[...].astype(BF)
        nwin = wb.shape[0]
        wpos = qpos - nwin + lax.broadcasted_iota(jnp.int32, (1, nwin), 1)
        wmask = (wpos <= qpos) & (wpos > qpos - NSA_WINDOW) & (wpos >= 0)
        sw = jnp.where(wmask, _dot_t(nqr_ref[...], wb), NEG)
        mw = jnp.max(sw, -1, keepdims=True)
        pw = jnp.where(wmask, jnp.exp2(sw - mw), 0.0)
        owin = _merge_new_key(nqr_ref[...], winn_ref[...], true8, mw,
                              jnp.sum(pw, -1, keepdims=True), _dot(pw.astype(BF), wb))
        ocmp = ocmp_ref[...]
        parts = []
        for h in range(HEADS):
            gt = [misc[:, M_GATE + HEADS * i + h:M_GATE + HEADS * i + h + 1] for i in range(3)]
            parts.append(gt[0] * ocmp[h:h + 1] + gt[1] * osel[h:h + 1] + gt[2] * owin[h:h + 1])
        o_ref[:, 2 * GROUP_WIDTH:3 * GROUP_WIDTH] = _place_heads(parts, lane[0:1])
        dnew = dmask_ref[:, new_at:new_at + 1] > 0.5
        od = _merge_new_key(sq_ref[...], kvn_ref[...], dnew, md[...], ld[...], accd[...])
        o_ref[:, 3 * GROUP_WIDTH:] = _place_heads([od[h:h + 1] for h in range(HEADS)], lane[0:1])


def _sample_attn(caches, layer, pt_flat, bd, n_pages, per_b_args, consts, lam_init, qpos, group):
    per_b = lambda b, g, pt: (b, 0, 0)

    def bs(a):
        return pl.BlockSpec((None,) + a.shape[1:], per_b)

    def cs(a):
        nd = a.ndim
        return pl.BlockSpec(a.shape, lambda b, g, pt: (0,) * nd)
    page_specs, page_args = [], []
    for c in caches:
        page_specs += _page_specs(c, layer, n_pages, group, 0)
        page_args += [c] * group
    wm = caches[0].shape[2]
    scratch = []
    for w in (wm, LANES, LANES, LANES):
        scratch += [pltpu.VMEM((8, 1), F32), pltpu.VMEM((8, 1), F32), pltpu.VMEM((8, w), F32)]
    gs = pltpu.PrefetchScalarGridSpec(
        num_scalar_prefetch=1,
        grid=(bd, n_pages // group),
        in_specs=page_specs + [bs(a) for a in per_b_args] + [cs(a) for a in consts],
        out_specs=pl.BlockSpec((None, 1, 4 * GROUP_WIDTH), per_b),
        scratch_shapes=scratch,
    )
    return pl.pallas_call(
        functools.partial(_sample_attn_body, group, n_pages, lam_init, qpos),
        out_shape=jax.ShapeDtypeStruct((bd, 1, 4 * GROUP_WIDTH), F32),
        grid_spec=gs,
        compiler_params=_params(("arbitrary", "arbitrary")),
        name="sample_mixers",
    )(pt_flat, *page_args, *per_b_args, *consts)


def _outproj_body(final, o_ref, z_ref, x_ref, mod_ref, w_ref, fn_ref, out_ref):
    d = x_ref.shape[-1]
    y = _dot((o_ref[...] * _silu(z_ref[...])).astype(BF), w_ref[...])
    xn = x_ref[...] + mod_ref[...][:, 2 * d:3 * d] * y
    if final:
        xn = xn * lax.rsqrt(jnp.mean(xn * xn, -1, keepdims=True) + EPS) * fn_ref[...]
    out_ref[...] = xn


def _outproj(o3, z3, x3, mod3, wout, fn, final, tm):
    bk, sk, d = x3.shape
    per_token = mod3.shape[1] != 1
    tok = lambda s, b: (b, s, 0)
    return pl.pallas_call(
        functools.partial(_outproj_body, final),
        out_shape=jax.ShapeDtypeStruct((bk, sk, d), F32),
        grid=(sk // tm, bk),
        in_specs=[pl.BlockSpec((None, tm, o3.shape[2]), tok), pl.BlockSpec((None, tm, z3.shape[2]), tok),
                  pl.BlockSpec((None, tm, d), tok),
                  pl.BlockSpec((None, tm, 3 * d), tok) if per_token
                  else pl.BlockSpec((None, 1, 3 * d), lambda s, b: (b, 0, 0)),
                  _const_spec(wout), _const_spec(fn)],
        out_specs=pl.BlockSpec((None, tm, d), tok),
        compiler_params=_params(("arbitrary", "arbitrary")),
        name="out_proj",
    )(o3, z3, x3, mod3, wout, fn)


def _in_columns():
    widths = (192, 128, 32, 256, 256, 64, 64, 256, 256, 128, 128, 128, 12, 256, 256, 128, 128, 4, 32, 256)
    names = ("mla_cq", "mla_ckv", "mla_kr", "mla_z", "diff_q", "diff_k", "diff_v", "diff_z", "nsa_q", "nsa_cmp",
             "nsa_sel", "nsa_win", "nsa_gate", "nsa_z", "dsa_q", "dsa_kv", "dsa_iq", "dsa_iw", "dsa_ik", "dsa_z")
    off, o = {}, 0
    for n, w in zip(names, widths):
        off[n] = (o, w)
        o += w

    def rng(n):
        return list(range(off[n][0], off[n][0] + off[n][1]))
    pad = lambda k: [-1] * k
    cols = (rng("mla_cq") + pad(64) + rng("mla_ckv")
            + rng("mla_kr") + rng("nsa_gate") + rng("dsa_iw") + pad(LANES - 48)
            + rng("mla_z") + rng("diff_z") + rng("nsa_z") + rng("dsa_z")
            + rng("diff_q") + rng("diff_k") + rng("diff_v") + rng("nsa_q") + rng("nsa_cmp") + rng("nsa_sel")
            + rng("nsa_win") + rng("dsa_q") + rng("dsa_kv") + rng("dsa_iq") + rng("dsa_ik") * 4)
    cols = np.asarray(cols, np.int32)
    assert cols.shape[0] == P_END and o == 2960
    return cols


def _rope_pattern(pos, width, period, rot, active):
    half = rot // 2
    j = np.arange(width)
    jj = j % period
    live = (j < active) & (jj < rot)
    inv = ROPE_THETA ** (-jnp.arange(half, dtype=F32) * 2.0 / rot)
    ang = pos.astype(F32)[:, None] * inv
    cos, sin = jnp.cos(ang), jnp.sin(ang)
    idx = jnp.asarray(jj % half)
    c = jnp.where(jnp.asarray(live), cos[:, idx], 1.0)
    sa = jnp.where(jnp.asarray(live & (jj < half)), -sin[:, idx], 0.0)
    sb = jnp.where(jnp.asarray(live & (jj >= half)), sin[:, idx], 0.0)
    return jnp.stack([c, sa, sb])


def _rope_tables(pos):
    t = [_rope_pattern(pos, 128, 32, 32, 128),
         _rope_pattern(pos, 256, 32, 8, 256),
         _rope_pattern(pos, 128, 32, 8, 64),
         _rope_pattern(pos, 256, 64, 16, 256),
         _rope_pattern(pos, 128, 64, 16, 64),
         _rope_pattern(pos, 128, 32, 32, 32)]
    return jnp.concatenate(t, axis=-1)


def _layer_weights(l, cols, w_ada, b_ada, norm_g, w_in, mla_q_norm, mla_kv_norm, mla_w_uq, mla_w_uk, mla_w_uv,
                   diff_lambda, diff_subln, nsa_cmp_pos, nsa_cmp_w1, nsa_cmp_w2, w_out):
    wl = {}
    wl["w_ada"] = w_ada[l].astype(BF)
    wl["b_ada"] = b_ada[l][None, :]
    wl["g"] = norm_g[l][None, :]
    wl["wr"] = jnp.where(jnp.asarray(cols >= 0)[None, :], w_in[l][:, np.maximum(cols, 0)], 0.0).astype(BF)
    wl["qn"] = jnp.pad(mla_q_norm[l], (0, 256 - MLA_Q_RANK))[None, :]
    wl["kvn"] = mla_kv_norm[l][None, :]
    uq = jnp.pad(mla_w_uq[l], ((0, 256 - MLA_Q_RANK), (0, 0), (0, 0)))
    wl["wuqn"] = uq[:, :, :MLA_NOPE].reshape(256, HEADS * MLA_NOPE).astype(BF)
    wl["wuqp"] = uq[:, :, MLA_NOPE:].reshape(256, HEADS * MLA_ROPE).astype(BF)
    wuk = jnp.zeros((HEADS * MLA_NOPE, HEADS * MLA_KV_RANK), F32)
    wuv = jnp.zeros((HEADS, MLA_KV_RANK, GROUP_WIDTH), F32)
    for h in range(HEADS):
        wuk = wuk.at[h * MLA_NOPE:(h + 1) * MLA_NOPE, h * MLA_KV_RANK:(h + 1) * MLA_KV_RANK].set(mla_w_uk[l][:, h, :].T)
        wuv = wuv.at[h, :, h * HEAD_DIM:(h + 1) * HEAD_DIM].set(mla_w_uv[l][:, h, :])
    wl["wukbd"] = wuk.astype(BF)
    wl["wuv"] = wuv.astype(BF)
    wl["subln"] = jnp.pad(diff_subln[l], (HEAD_DIM, 0))[None, :]
    wl["lam"] = diff_lambda[l]
    w1 = nsa_cmp_w1[l].reshape(2, 2, NSA_CMP_STRIDE, NSA_DK, NSA_DK)
    w1s = jnp.zeros((NSA_CMP_STRIDE, 2, NSA_DK, 2, 2, NSA_DK), F32)
    for c in range(2):
        w1s = w1s.at[:, c, :, :, c, :].set(jnp.transpose(w1[c], (1, 2, 0, 3)))
    wl["w1s"] = w1s.reshape(NSA_CMP_STRIDE, 2 * NSA_DK, 4 * NSA_DK).astype(BF)
    pos = nsa_cmp_pos[l].reshape(2, 2, NSA_CMP_STRIDE, NSA_DK)
    pos8 = jnp.transpose(pos, (2, 1, 0, 3)).reshape(NSA_CMP_STRIDE, 2, 2 * NSA_DK)
    wl["pos8"] = jnp.pad(pos8, ((0, 0), (0, 6), (0, 0)))
    w2 = jnp.zeros((2 * NSA_DK, 2 * NSA_DK), F32)
    for c in range(2):
        w2 = w2.at[c * NSA_DK:(c + 1) * NSA_DK, c * NSA_DK:(c + 1) * NSA_DK].set(nsa_cmp_w2[l][c])
    wl["w2bd"] = w2.astype(BF)
    wl["wout"] = w_out[l].astype(BF)
    return wl


def _overlap(n_cmp, n_rows, n_sel, n_cols):
    i = np.arange(n_rows)[:, None]
    j = np.arange(n_cols)[None, :]
    cs, ss = i * NSA_CMP_STRIDE, j * NSA_SEL_BLOCK
    ov = (cs <= ss + NSA_SEL_BLOCK - 1) & (cs + NSA_CMP_BLOCK - 1 >= ss) & (i < n_cmp) & (j < n_sel)
    return ov.astype(np.float32)


def _round_up(x, m):
    return -(-x // m) * m


def kernel(x_prompt, x_sample, cache_mla, cache_diff, cache_nsa_cmp, cache_nsa_sel, cache_dsa_kv, cache_dsa_idx,
           state_nsa_win, page_table, c_prompt, c_sample, w_ada, b_ada, norm_g, w_in, mla_q_norm, mla_kv_norm,
           mla_w_uq, mla_w_uk, mla_w_uv, diff_lambda, diff_subln, nsa_cmp_pos, nsa_cmp_w1, nsa_cmp_w2, w_out,
           final_norm):
    b, s, d = x_prompt.shape
    bd, s_dec, _ = x_sample.shape
    depth = w_in.shape[0]
    n_pages, page = page_table.shape[1], cache_mla.shape[2]
    past = n_pages * page
    assert s_dec == 1 and page == LANES and state_nsa_win.shape[2] == NSA_WINDOW
    tq = next(t for t in (256, 128) if s % t == 0)
    tm_p = next(t for t in (512, 256, 128) if s % t == 0)
    grp_cmp = next(t for t in (64, 32, 16, 8, 4, 2, 1) if n_pages % t == 0)
    assert s % tq == 0 and s >= NSA_WINDOW + tq and bd % 8 == 0 and past % NSA_SEL_BLOCK == 0
    grp = min(64, n_pages)
    n_bucket = s // tq
    assert n_pages % grp == 0 and s % (n_bucket * tq) == 0
    cache_mla_t = jnp.swapaxes(cache_mla, 2, 3)
    cache_idx_t = jnp.swapaxes(cache_dsa_idx, 2, 3)

    cols = _in_columns()
    tab_p = _rope_tables(jnp.arange(s))
    tab_s = jnp.broadcast_to(_rope_tables(past + jnp.arange(1)), (3, bd, T_END))
    pt_flat = page_table.reshape(-1)
    fn = final_norm[None, :]

    nch_p = s // NSA_CMP_STRIDE
    ncmp_p = nch_p - NSA_CMP_BLOCK // NSA_CMP_STRIDE + 1
    nsel_p = -(-s // NSA_SEL_BLOCK)
    nsp = _round_up(nsel_p, 8)
    ovt = jnp.asarray(_overlap(ncmp_p, nch_p, nsel_p, nsp).T, BF)
    emat = jnp.asarray((np.arange(s)[None, :] // NSA_SEL_BLOCK == np.arange(nsp)[:, None]).astype(np.float32), BF)
    ntop_p, ksel_p = min(NSA_TOPN, nsel_p), min(DSA_TOPK, s // 4)
    l_s = past + 1
    nch_s = l_s // NSA_CMP_STRIDE
    ncmp_s = nch_s - NSA_CMP_BLOCK // NSA_CMP_STRIDE + 1
    nsel_s = -(-l_s // NSA_SEL_BLOCK)
    nsl = _round_up(nsel_s, LANES)
    ov_s = jnp.asarray(_overlap(ncmp_s, nch_s, nsel_s, nsl), BF)
    ntop_s, ksel_s = min(NSA_TOPN, nsel_s), min(DSA_TOPK, l_s // 4)

    xp = x_prompt
    xs = x_sample.reshape(1, bd, d)
    rows_p, rows_s = [], []
    for l in range(depth):
        wl = _layer_weights(l, cols, w_ada, b_ada, norm_g, w_in, mla_q_norm, mla_kv_norm, mla_w_uq, mla_w_uk,
                            mla_w_uv, diff_lambda, diff_subln, nsa_cmp_pos, nsa_cmp_w1, nsa_cmp_w2, w_out)
        lam_init = 0.8 - 0.6 * math.exp(-0.3 * l)
        final = l == depth - 1
        shared = (wl["g"], wl["wr"], wl["qn"], wl["kvn"], wl["wuqn"], wl["wuqp"], wl["wukbd"])

        mod_p = _ada(c_prompt, wl["w_ada"], wl["b_ada"])[:, None, :]
        pr = _inproj(xp, mod_p, *shared, tab_p, tm_p)
        cmpkv_p = _cmp2(_cmp1_prompt(pr["cmp"], wl["w1s"]), wl["pos8"], wl["w1s"], wl["w2bd"])
        o_p = _prompt_attn(pr, cmpkv_p, wl["wuv"], wl["subln"], wl["lam"], ovt, emat, lam_init,
                           ncmp_p, nsel_p, ntop_p, ksel_p, tq, n_bucket)
        xp = _outproj(o_p, pr["z"], xp, mod_p, wl["wout"], fn, final, tm_p)
        rows_p.append(pr)

        mod_s = _ada(c_sample, wl["w_ada"], wl["b_ada"])[None]
        sr = _inproj(xs, mod_s, *shared, tab_s, bd)
        cmpkv_s = _cmp2(_cmp1_sample(cache_nsa_cmp, l, pt_flat, bd, n_pages, wl["w1s"], grp_cmp),
                        wl["pos8"], wl["w1s"], wl["w2bd"])

        def rows8(a):
            t = jnp.transpose(a[:, 0], (1, 0, 2))
            return jnp.pad(t, ((0, 0), (0, 8 - t.shape[1]), (0, 0)))
        iq8 = jnp.stack([sr["iq"][h, 0, :, h * DSA_IDX_DIM:(h + 1) * DSA_IDX_DIM] for h in range(HEADS)], axis=1)
        iq8 = jnp.pad(iq8, ((0, 0), (0, 8 - HEADS), (0, 0)))
        misc_s = sr["misc"][0]
        iw8 = jnp.pad(misc_s[:, M_IW:M_IW + HEADS], ((0, 0), (0, 8 - HEADS)))[:, :, None]
        tok = lambda name: sr[name][0][:, None, :]
        dmask, smask, ocmp = _sample_select(
            cache_idx_t, l, pt_flat, bd, n_pages, iq8, iw8, tok("idx"), rows8(sr["nq"]), cmpkv_s, ov_s,
            ncmp_s, nsel_s, ntop_s, ksel_s, past, n_pages)
        flat = lambda m: m.reshape(bd, 1, -1)
        per_b = [rows8(sr["qa"]), rows8(sr["dq"]), rows8(sr["nqr"]), rows8(sr["sq"]), misc_s[:, None, :],
                 ocmp, flat(smask), flat(dmask), tok("mla"), tok("diff"), tok("sel"), tok("win"), tok("kv"),
                 state_nsa_win[l]]
        o_s = _sample_attn((cache_mla_t, cache_diff, cache_nsa_sel, cache_dsa_kv), l, pt_flat, bd, n_pages,
                           per_b, [wl["wuv"], wl["subln"], wl["lam"]], lam_init, past, grp)
        xs = _outproj(o_s.reshape(1, bd, 4 * GROUP_WIDTH), sr["z"], xs, mod_s, wl["wout"], fn, final, bd)
        rows_s.append(sr)

    def stack_p(name):
        return jnp.stack([r[name] for r in rows_p])

    def stack_s(name):
        return jnp.stack([r[name][0][:, None, :] for r in rows_s])
    win_p = jnp.stack([r["win"][:, s - min(NSA_WINDOW, s):] for r in rows_p])
    win_s = jnp.stack([jnp.concatenate([state_nsa_win[l][:, 1:], rows_s[l]["win"][0][:, None, :]], axis=1)
                       for l in range(depth)])
    return (xp, xs.reshape(bd, 1, d),
            stack_p("mla"), stack_s("mla"), stack_p("diff"), stack_s("diff"),
            stack_p("cmp"), stack_s("cmp"), stack_p("sel"), stack_s("sel"),
            stack_p("kv"), stack_s("kv"), stack_p("idx"), stack_s("idx"), win_p, win_s)
```
